```python
import math
import jax, jax.numpy as jnp
from jax import lax
import numpy as np

D_MODEL = 1024
BATCH = 4
SEQ = 4096
DEPTH = 2

HEAD_DIM = 64
H_FOX = 4
H_DIFF = 4
H_STICK = 4
DIFF_V_DIM = 2 * HEAD_DIM
W_FOX = H_FOX * HEAD_DIM
W_DIFF_QK = 2 * H_DIFF * HEAD_DIM
W_DIFF = H_DIFF * DIFF_V_DIM
W_STICK = H_STICK * HEAD_DIM
N_BRANCH = 3
ROPE_THETA = 500000.0
ROPE_DIM = HEAD_DIM // 4
D_FF = 2816
CONV_WIDTH = 3
Q_BLOCK = 128
NORM_EPS = 1e-6
SPLITS = (W_FOX, W_FOX, W_FOX, W_DIFF_QK, W_DIFF_QK, W_DIFF, W_STICK, W_STICK, W_STICK, H_FOX, N_BRANCH * D_MODEL)
N_IN = 3 * W_FOX + 2 * W_DIFF_QK + W_DIFF + 3 * W_STICK + H_FOX + N_BRANCH * D_MODEL

kernel_name = "hybrid_fox_diff_stickbreaking_convffn"


def rmsnorm(x, g):
    xf = x.astype(jnp.float32)
    y = xf * lax.rsqrt(jnp.mean(xf * xf, axis=-1, keepdims=True) + NORM_EPS)
    return (y * g.astype(jnp.float32)).astype(x.dtype)


def heads(t, n, dh):
    b, s, _ = t.shape
    return t.reshape(b, s, n, dh).transpose(0, 2, 1, 3)


def merge_heads(t):
    b, n, s, dh = t.shape
    return t.transpose(0, 2, 1, 3).reshape(b, s, n * dh)


def split_blocks(t, axis):
    shp = t.shape
    t = t.reshape(shp[:axis] + (shp[axis] // Q_BLOCK, Q_BLOCK) + shp[axis + 1:])
    return jnp.moveaxis(t, axis, 0)


def merge_blocks(o, axis):
    o = jnp.moveaxis(o, 0, axis)
    shp = o.shape
    return o.reshape(shp[:axis] + (shp[axis] * shp[axis + 1],) + shp[axis + 2:])


def partial_rope(t, cos, sin):
    half = ROPE_DIM // 2
    t1 = t[..., :half]
    t2 = t[..., half:ROPE_DIM]
    return jnp.concatenate([t1 * cos - t2 * sin, t2 * cos + t1 * sin, t[..., ROPE_DIM:]], axis=-1)


def fox_attention(q, k, v, log_f):
    s_len = q.shape[2]
    c = jnp.cumsum(log_f, axis=-1)
    kpos = jnp.arange(s_len)
    scale = HEAD_DIM ** -0.5

    def one_block(args):
        qb, cq, i = args
        qpos = i * Q_BLOCK + jnp.arange(Q_BLOCK)
        s = jnp.einsum('bhqd,bhkd->bhqk', qb, k).astype(jnp.float32) * scale
        s = s + cq[..., :, None] - c[..., None, :]
        s = jnp.where(kpos[None, :] <= qpos[:, None], s, -jnp.inf)
        p = jax.nn.softmax(s, axis=-1).astype(v.dtype)
        return jnp.einsum('bhqk,bhkd->bhqd', p, v)

    out = lax.map(one_block, (split_blocks(q, 2), split_blocks(c, 2), jnp.arange(s_len // Q_BLOCK)))
    return merge_blocks(out, 2)


def diff_attention(q, k, v, lam):
    s_len = q.shape[3]
    kpos = jnp.arange(s_len)
    scale = HEAD_DIM ** -0.5

    def one_block(args):
        qb, i = args
        qpos = i * Q_BLOCK + jnp.arange(Q_BLOCK)
        s = jnp.einsum('bhcqd,bhckd->bhcqk', qb, k).astype(jnp.float32) * scale
        s = jnp.where(kpos[None, :] <= qpos[:, None], s, -jnp.inf)
        p = jax.nn.softmax(s, axis=-1)
        a = p[:, :, 0] - lam * p[:, :, 1]
        return jnp.einsum('bhqk,bhkd->bhqd', a.astype(v.dtype), v)

    out = lax.map(one_block, (split_blocks(q, 3), jnp.arange(s_len // Q_BLOCK)))
    return merge_blocks(out, 2)


def stick_breaking_attention(q, k, v):
    s_len = q.shape[2]
    kpos = jnp.arange(s_len)
    scale = HEAD_DIM ** -0.5

    def one_block(args):
        qb, i = args
        qpos = i * Q_BLOCK + jnp.arange(Q_BLOCK)
        z = jnp.einsum('bhqd,bhkd->bhqk', qb, k).astype(jnp.float32) * scale
        strict = kpos[None, :] < qpos[:, None]
        log_1m = jnp.where(strict, jax.nn.log_sigmoid(-z), 0.0)
        after = lax.cumsum(log_1m, axis=3, reverse=True) - log_1m
        w = jnp.where(strict, jnp.exp(jax.nn.log_sigmoid(z) + after), 0.0)
        return jnp.einsum('bhqk,bhkd->bhqd', w.astype(v.dtype), v)

    out = lax.map(one_block, (split_blocks(q, 2), jnp.arange(s_len // Q_BLOCK)))
    return merge_blocks(out, 2)


def causal_depthwise_conv(x, w, b):
    ch = x.shape[-1]
    y = lax.conv_general_dilated(x, w[:, None, :].astype(x.dtype), window_strides=(1,),
                                 padding=[(CONV_WIDTH - 1, 0)],
                                 dimension_numbers=('NWC', 'WIO', 'NWC'),
                                 feature_group_count=ch)
    return y + b


def setup_inputs(seed: int = 0) -> dict:
    key = jax.random.key(seed)
    ks = jax.random.split(key, 20)
    f32 = jnp.float32
    nrm = lambda k, shp, s: jax.random.normal(k, shp, f32) * s
    return {
        'x': nrm(ks[0], (BATCH, SEQ, D_MODEL), 1.0),
        'attn_norm_g': 1.0 + nrm(ks[1], (DEPTH, D_MODEL), 0.02),
        'w_in': nrm(ks[2], (DEPTH, D_MODEL, N_IN), D_MODEL ** -0.5),
        'forget_bias': 2.0 + nrm(ks[3], (DEPTH, H_FOX), 0.5),
        'lam_q1': nrm(ks[4], (DEPTH, HEAD_DIM), 0.1),
        'lam_k1': nrm(ks[5], (DEPTH, HEAD_DIM), 0.1),
        'lam_q2': nrm(ks[6], (DEPTH, HEAD_DIM), 0.1),
        'lam_k2': nrm(ks[7], (DEPTH, HEAD_DIM), 0.1),
        'diff_subln_g': 1.0 + nrm(ks[8], (DEPTH, DIFF_V_DIM), 0.02),
        'w_br_fox': nrm(ks[9], (DEPTH, W_FOX, D_MODEL), W_FOX ** -0.5),
        'w_br_diff': nrm(ks[10], (DEPTH, W_DIFF, D_MODEL), W_DIFF ** -0.5),
        'w_br_stick': nrm(ks[11], (DEPTH, W_STICK, D_MODEL), W_STICK ** -0.5),
        'w_out': nrm(ks[12], (DEPTH, D_MODEL, D_MODEL), D_MODEL ** -0.5),
        'ffn_norm_g': 1.0 + nrm(ks[13], (DEPTH, D_MODEL), 0.02),
        'w_up': nrm(ks[14], (DEPTH, D_MODEL, 2 * D_FF), D_MODEL ** -0.5),
        'conv_w': nrm(ks[15], (DEPTH, CONV_WIDTH, 2 * D_FF), CONV_WIDTH ** -0.5),
        'conv_b': nrm(ks[16], (DEPTH, 2 * D_FF), 0.02),
        'w_down': nrm(ks[17], (DEPTH, D_FF, D_MODEL), D_FF ** -0.5),
        'final_norm_g': 1.0 + nrm(ks[18], (D_MODEL,), 0.02),
    }


def reference(x, attn_norm_g, w_in, forget_bias, lam_q1, lam_k1, lam_q2, lam_k2, diff_subln_g,
              w_br_fox, w_br_diff, w_br_stick, w_out, ffn_norm_g, w_up, conv_w, conv_b, w_down,
              final_norm_g):
    f32 = jnp.float32
    b, s_len, _ = x.shape
    pos = jnp.arange(s_len, dtype=f32)
    inv_freq = ROPE_THETA ** (-jnp.arange(0, ROPE_DIM, 2, dtype=f32) / ROPE_DIM)
    ang = pos[:, None] * inv_freq[None, :]
    cos = jnp.cos(ang).astype(x.dtype)
    sin = jnp.sin(ang).astype(x.dtype)
    offs = np.cumsum(SPLITS)[:-1].tolist()

    for l in range(DEPTH):
        xn = rmsnorm(x, attn_norm_g[l])
        proj = xn @ w_in[l]
        fq, fk, fv, dq, dk, dv, sq, sk, sv, f_logit, g_logit = jnp.split(proj, offs, axis=-1)

        log_f = jax.nn.log_sigmoid(f_logit.astype(f32) + forget_bias[l].astype(f32)).transpose(0, 2, 1)
        o_fox = fox_attention(heads(fq, H_FOX, HEAD_DIM), heads(fk, H_FOX, HEAD_DIM),
                              heads(fv, H_FOX, HEAD_DIM), log_f)

        dq = partial_rope(dq.reshape(b, s_len, H_DIFF, 2, HEAD_DIM).transpose(0, 2, 3, 1, 4), cos, sin)
        dk = partial_rope(dk.reshape(b, s_len, H_DIFF, 2, HEAD_DIM).transpose(0, 2, 3, 1, 4), cos, sin)
        lam_init = 0.8 - 0.6 * math.exp(-0.3 * l)
        lam = (jnp.exp(jnp.sum(lam_q1[l].astype(f32) * lam_k1[l].astype(f32)))
               - jnp.exp(jnp.sum(lam_q2[l].astype(f32) * lam_k2[l].astype(f32))) + lam_init)
        o_diff = diff_attention(dq, dk, heads(dv, H_DIFF, DIFF_V_DIM), lam)
        o_diff = rmsnorm(o_diff, diff_subln_g[l]) * (1.0 - lam_init)

        o_stick = stick_breaking_attention(heads(sq, H_STICK, HEAD_DIM), heads(sk, H_STICK, HEAD_DIM),
                                           heads(sv, H_STICK, HEAD_DIM))

        gates = jax.nn.sigmoid(g_logit.astype(f32)).astype(x.dtype).reshape(b, s_len, N_BRANCH, D_MODEL)
        mixed = (gates[:, :, 0] * (merge_heads(o_fox) @ w_br_fox[l])
                 + gates[:, :, 1] * (merge_heads(o_diff) @ w_br_diff[l])
                 + gates[:, :, 2] * (merge_heads(o_stick) @ w_br_stick[l]))
        x = x + mixed @ w_out[l]

        hn = rmsnorm(x, ffn_norm_g[l])
        u = causal_depthwise_conv(hn @ w_up[l], conv_w[l], conv_b[l])
        u_gate, u_val = jnp.split(u, 2, axis=-1)
        x = x + (jax.nn.silu(u_gate) * u_val) @ w_down[l]

    return rmsnorm(x, final_norm_g)
```

```python
import functools
import math

import jax
import jax.numpy as jnp
from jax import lax
from jax.experimental import pallas as pl
from jax.experimental.pallas import tpu as pltpu

F32 = jnp.float32
BF16 = jnp.bfloat16

D_MODEL = 1024
HEAD_DIM = 64
H_FOX = 4
H_DIFF = 4
H_STICK = 4
W_FOX = H_FOX * HEAD_DIM
W_DIFF_QK = 2 * H_DIFF * HEAD_DIM
W_DIFF = H_DIFF * 2 * HEAD_DIM
W_STICK = H_STICK * HEAD_DIM
N_BRANCH = 3
ROPE_THETA = 500000.0
ROPE_DIM = HEAD_DIM // 4
D_FF = 2816
CONV_WIDTH = 3
NORM_EPS = 1e-6
QK_SCALE = HEAD_DIM ** -0.5

LANES = 128
SUBLANES = 8
VMEM_LIMIT_BYTES = 56 * 1024 * 1024

OFF_FQ = 0
OFF_FK = OFF_FQ + W_FOX
OFF_FV = OFF_FK + W_FOX
OFF_DQ = OFF_FV + W_FOX
OFF_DK = OFF_DQ + W_DIFF_QK
OFF_DV = OFF_DK + W_DIFF_QK
OFF_SQ = OFF_DV + W_DIFF
OFF_SK = OFF_SQ + W_STICK
OFF_SV = OFF_SK + W_STICK
OFF_GATE = OFF_SV + W_STICK
N_SLAB = OFF_GATE + N_BRANCH * D_MODEL

TOKEN_TILE = 512
ATTN_TILE = 256
DECAY_ROWS = 16
FF_CHUNK = 256


def _log_sigmoid_parts(z):
    l1p = jnp.log(1.0 + jnp.exp(-jnp.abs(z)))
    return jnp.minimum(z, 0.0) - l1p, -jnp.maximum(z, 0.0) - l1p


def _rms_scale(x):
    return lax.rsqrt(jnp.mean(x * x, axis=-1, keepdims=True) + NORM_EPS)


def _split_bf16(v, n_parts):
    parts = []
    rem = v
    for _ in range(n_parts):
        p = rem.astype(BF16)
        parts.append(p)
        rem = rem - p.astype(F32)
    return parts


def _dot_nt(a, b):
    return lax.dot_general(a, b, (((1,), (1,)), ((), ())), preferred_element_type=F32)


def _dot(a, b):
    return jnp.dot(a, b, preferred_element_type=F32)


_PROJ_CHUNKS = (
    [(OFF_FQ, 3 * W_FOX // 2, "plain"), (OFF_FQ + 3 * W_FOX // 2, 3 * W_FOX // 2, "plain")]
    + [(OFF_DQ, W_DIFF_QK, "rope"), (OFF_DK, W_DIFF_QK, "rope"), (OFF_DV, W_DIFF, "plain")]
    + [(OFF_SQ, 3 * W_STICK // 2, "plain"), (OFF_SQ + 3 * W_STICK // 2, 3 * W_STICK // 2, "plain")]
    + [(OFF_GATE + c * 512, 512, "sigmoid") for c in range(N_BRANCH * D_MODEL // 512)]
)


def _rope_lanes(t, cos, sin_up, sin_dn):
    return (t * cos + pltpu.roll(t, ROPE_DIM // 2, 1) * sin_up
            + pltpu.roll(t, LANES - ROPE_DIM // 2, 1) * sin_dn)


def _inproj_kernel(x_ref, g_ref, w_ref, wf_ref, fb_ref, cos_ref, sup_ref, sdn_ref,
                   slab_ref, logf_ref, xn_ref):
    x = x_ref[...]
    xn_ref[...] = ((x * _rms_scale(x)) * g_ref[...]).astype(BF16)
    for off, width, kind in _PROJ_CHUNKS:
        res = _dot(xn_ref[...], w_ref[:, off:off + width])
        if kind == "rope":
            cos, sup, sdn = cos_ref[...], sup_ref[...], sdn_ref[...]
            for c in range(width // LANES):
                blk = _rope_lanes(res[:, c * LANES:(c + 1) * LANES], cos, sup, sdn)
                slab_ref[:, off + c * LANES:off + (c + 1) * LANES] = blk.astype(BF16)
        elif kind == "sigmoid":
            slab_ref[:, off:off + width] = jax.nn.sigmoid(res).astype(BF16)
        else:
            slab_ref[:, off:off + width] = res.astype(BF16)
    f_logit = _dot(xn_ref[...], wf_ref[...]) + fb_ref[...]
    logf_ref[...] = _log_sigmoid_parts(f_logit)[0]


def _inproj(x2d, g, w_slab, w_f, f_bias, cos_t, sup_t, sdn_t, seq_len):
    n_tok = x2d.shape[0]
    tiles_per_seq = seq_len // TOKEN_TILE
    const = lambda i: (0, 0)
    rope_map = lambda i: (i % tiles_per_seq, 0)
    return pl.pallas_call(
        _inproj_kernel,
        grid=(n_tok // TOKEN_TILE,),
        in_specs=[
            pl.BlockSpec((TOKEN_TILE, D_MODEL), lambda i: (i, 0)),
            pl.BlockSpec((1, D_MODEL), const),
            pl.BlockSpec((D_MODEL, N_SLAB), const, pipeline_mode=pl.Buffered(1)),
            pl.BlockSpec((D_MODEL, LANES), const, pipeline_mode=pl.Buffered(1)),
            pl.BlockSpec((1, LANES), const),
            pl.BlockSpec((TOKEN_TILE, LANES), rope_map),
            pl.BlockSpec((TOKEN_TILE, LANES), rope_map),
            pl.BlockSpec((TOKEN_TILE, LANES), rope_map),
        ],
        out_specs=[
            pl.BlockSpec((TOKEN_TILE, N_SLAB), lambda i: (i, 0)),
            pl.BlockSpec((TOKEN_TILE, LANES), lambda i: (i, 0)),
        ],
        out_shape=[
            jax.ShapeDtypeStruct((n_tok, N_SLAB), BF16),
            jax.ShapeDtypeStruct((n_tok, LANES), F32),
        ],
        scratch_shapes=[pltpu.VMEM((TOKEN_TILE, D_MODEL), BF16)],
        compiler_params=pltpu.CompilerParams(
            dimension_semantics=("arbitrary",), vmem_limit_bytes=VMEM_LIMIT_BYTES),
        name="inproj",
    )(x2d, g, w_slab, w_f, f_bias, cos_t, sup_t, sdn_t)


def _decay_kernel(logf_ref, negc_ref):
    t = ATTN_TILE
    seq_len = logf_ref.shape[-1]
    row = lax.broadcasted_iota(jnp.int32, (t, t), 0)
    col = lax.broadcasted_iota(jnp.int32, (t, t), 1)
    tri = (row <= col).astype(BF16)
    carry = jnp.zeros((DECAY_ROWS, 1), F32)
    for blk in range(seq_len // t):
        parts = _split_bf16(logf_ref[0, 0, :, blk * t:(blk + 1) * t], 3)
        cs = _dot(parts[0], tri) + _dot(parts[1], tri) + _dot(parts[2], tri) + carry
        negc_ref[0, 0, blk] = -cs
        carry = cs[:, t - 1:t]


def _decay(logf_t):
    b, pairs, rows, seq_len = logf_t.shape
    n_tiles = seq_len // ATTN_TILE
    return pl.pallas_call(
        _decay_kernel,
        grid=(b, pairs),
        in_specs=[pl.BlockSpec((1, 1, rows, seq_len), lambda i, j: (i, j, 0, 0))],
        out_specs=pl.BlockSpec((1, 1, n_tiles, rows, ATTN_TILE), lambda i, j: (i, j, 0, 0, 0)),
        out_shape=jax.ShapeDtypeStruct((b, pairs, n_tiles, rows, ATTN_TILE), F32),
        compiler_params=pltpu.CompilerParams(dimension_semantics=("arbitrary", "arbitrary")),
        name="fox_decay",
    )(logf_t)


def _stack_halves(q):
    lane = lax.broadcasted_iota(jnp.int32, q.shape, 1)
    zero = jnp.zeros_like(q)
    return jnp.concatenate(
        [jnp.where(lane < HEAD_DIM, q, zero), jnp.where(lane >= HEAD_DIM, q, zero)], axis=0)


def _merge_halves(top, bottom):
    lane = lax.broadcasted_iota(jnp.int32, top.shape, 1)
    return jnp.where(lane < HEAD_DIM, top, bottom)


def _causal_keep(strict):
    row = lax.broadcasted_iota(jnp.int32, (ATTN_TILE, ATTN_TILE), 0)
    col = lax.broadcasted_iota(jnp.int32, (ATTN_TILE, ATTN_TILE), 1)
    return col < row if strict else col <= row


def _softmax_tile_update(s, v, rows, m_ref, l_ref, acc_ref):
    m_prev = m_ref[rows, :]
    m_new = jnp.maximum(m_prev, jnp.max(s, axis=-1, keepdims=True))
    alpha = jnp.exp(m_prev - m_new)
    p = jnp.exp(s - m_new)
    l_ref[rows, :] = alpha * l_ref[rows, :] + jnp.sum(p, axis=-1, keepdims=True)
    acc_ref[rows, :] = alpha * acc_ref[rows, :] + _dot(p.astype(BF16), v)
    m_ref[rows, :] = m_new


def _softmax_attention_body(q_ref, k_ref, v_ref, bias_fn, m_ref, l_ref, acc_ref):
    i = pl.program_id(2)
    t = ATTN_TILE
    q2 = _stack_halves(q_ref[0])
    m_ref[...] = jnp.full(m_ref.shape, -jnp.inf, F32)
    l_ref[...] = jnp.zeros(l_ref.shape, F32)
    acc_ref[...] = jnp.zeros(acc_ref.shape, F32)

    def tile(j, diagonal):
        ks = pl.ds(pl.multiple_of(j * t, t), t)
        s2 = _dot_nt(q2, k_ref[0, ks, :])
        v = v_ref[0, ks, :]
        for half in range(2):
            rows = slice(half * t, (half + 1) * t)
            s = s2[rows, :]
            if bias_fn is not None:
                s = s + bias_fn(half, j)
            if diagonal:
                s = jnp.where(_causal_keep(strict=False), s, -jnp.inf)
            _softmax_tile_update(s, v, rows, m_ref, l_ref, acc_ref)

    def off_diagonal(j, carry):
        tile(j, diagonal=False)
        return carry

    lax.fori_loop(0, i, off_diagonal, 0)
    tile(i, diagonal=True)


def _fox_kernel(q_ref, k_ref, v_ref, negc_ref, o_ref, m_ref, l_ref, acc_ref):
    bias_fn = lambda half, j: negc_ref[0, 0, j, half:half + 1, :]
    _softmax_attention_body(q_ref, k_ref, v_ref, bias_fn, m_ref, l_ref, acc_ref)
    t = ATTN_TILE
    o = acc_ref[...] / l_ref[...]
    o_ref[0] = _merge_halves(o[:t], o[t:]).astype(o_ref.dtype)


def _diff_kernel(lam_init, q_ref, k_ref, v_ref, lam_ref, g_ref, o_ref, m_ref, l_ref, acc_ref):
    _softmax_attention_body(q_ref, k_ref, v_ref, None, m_ref, l_ref, acc_ref)
    t = ATTN_TILE
    lp = lam_ref[...]
    lam = (jnp.exp(jnp.sum(lp[0:1] * lp[1:2], axis=-1, keepdims=True))
           - jnp.exp(jnp.sum(lp[2:3] * lp[3:4], axis=-1, keepdims=True)) + lam_init)
    o = acc_ref[...] / l_ref[...]
    o = o[:t] - lam * o[t:]
    o = (o * _rms_scale(o)) * g_ref[...]
    o_ref[0] = (o * (1.0 - lam_init)).astype(o_ref.dtype)


def _stick_kernel(q_ref, k_ref, v_ref, o_ref, r_ref, acc_ref):
    i = pl.program_id(2)
    t = ATTN_TILE
    q2 = _stack_halves(q_ref[0])
    r_ref[...] = jnp.zeros(r_ref.shape, F32)
    acc_ref[...] = jnp.zeros(acc_ref.shape, F32)
    row = lax.broadcasted_iota(jnp.int32, (t, t), 0)
    col = lax.broadcasted_iota(jnp.int32, (t, t), 1)
    later = (row > col).astype(BF16)

    def tile(j, diagonal):
        ks = pl.ds(pl.multiple_of(j * t, t), t)
        z2 = _dot_nt(q2, k_ref[0, ks, :])
        v = v_ref[0, ks, :]
        for half in range(2):
            rows = slice(half * t, (half + 1) * t)
            log_b, log_1m = _log_sigmoid_parts(z2[rows, :])
            if diagonal:
                log_1m = jnp.where(_causal_keep(strict=True), log_1m, 0.0)
            hi, lo = _split_bf16(log_1m, 2)
            after = _dot(hi, later) + _dot(lo, later) + r_ref[rows, :]
            w = jnp.exp(log_b + after)
            if diagonal:
                w = jnp.where(_causal_keep(strict=True), w, 0.0)
            acc_ref[rows, :] += _dot(w.astype(BF16), v)
            r_ref[rows, :] += jnp.sum(log_1m, axis=-1, keepdims=True)

    tile(i, diagonal=True)

    def off_diagonal(n, carry):
        tile(i - 1 - n, diagonal=False)
        return carry

    lax.fori_loop(0, i, off_diagonal, 0)
    acc = acc_ref[...]
    o_ref[0] = _merge_halves(acc[:t], acc[t:]).astype(o_ref.dtype)


def _attention_call(kernel, slab, extra_inputs, extra_specs, n_blocks, q_blk, k_blk, v_blk,
                    scratch, name):
    b, seq_len, _ = slab.shape
    t = ATTN_TILE
    q_spec = pl.BlockSpec((1, t, LANES), lambda bi, h, i: (bi, i, q_blk + h))
    k_spec = pl.BlockSpec((1, seq_len, LANES), lambda bi, h, i: (bi, 0, k_blk + h))
    v_spec = pl.BlockSpec((1, seq_len, LANES), lambda bi, h, i: (bi, 0, v_blk + h))
    return pl.pallas_call(
        kernel,
        grid=(b, n_blocks, seq_len // t),
        in_specs=[q_spec, k_spec, v_spec] + extra_specs,
        out_specs=pl.BlockSpec((1, t, LANES), lambda bi, h, i: (bi, i, h)),
        out_shape=jax.ShapeDtypeStruct((b, seq_len, n_blocks * LANES), BF16),
        scratch_shapes=scratch,
        compiler_params=pltpu.CompilerParams(
            dimension_semantics=("arbitrary", "arbitrary", "arbitrary"),
            vmem_limit_bytes=VMEM_LIMIT_BYTES),
        name=name,
    )(slab, slab, slab, *extra_inputs)


def _softmax_scratch():
    t2 = 2 * ATTN_TILE
    return [pltpu.VMEM((t2, 1), F32), pltpu.VMEM((t2, 1), F32), pltpu.VMEM((t2, LANES), F32)]


def _fox_attention(slab, negc):
    negc_spec = pl.BlockSpec((1, 1) + negc.shape[2:], lambda bi, h, i: (bi, h, 0, 0, 0))
    return _attention_call(
        _fox_kernel, slab, [negc], [negc_spec], H_FOX // 2,
        OFF_FQ // LANES, OFF_FK // LANES, OFF_FV // LANES, _softmax_scratch(), "fox_attention")


def _diff_attention(slab, lam_params, subln_g, lam_init):
    const = lambda bi, h, i: (0, 0)
    specs = [pl.BlockSpec(lam_params.shape, const), pl.BlockSpec(subln_g.shape, const)]
    return _attention_call(
        functools.partial(_diff_kernel, lam_init), slab, [lam_params, subln_g], specs, H_DIFF,
        OFF_DQ // LANES, OFF_DK // LANES, OFF_DV // LANES, _softmax_scratch(), "diff_attention")


def _stick_attention(slab):
    t2 = 2 * ATTN_TILE
    scratch = [pltpu.VMEM((t2, 1), F32), pltpu.VMEM((t2, LANES), F32)]
    return _attention_call(
        _stick_kernel, slab, [], [], H_STICK // 2,
        OFF_SQ // LANES, OFF_SK // LANES, OFF_SV // LANES, scratch, "stick_attention")


def _merge_kernel(x_ref, gate_ref, of_ref, od_ref, os_ref, wf_ref, wd_ref, ws_ref, wo_ref, y_ref):
    d = D_MODEL
    mixed = (gate_ref[:, 0:d].astype(F32) * _dot(of_ref[...], wf_ref[...])
             + gate_ref[:, d:2 * d].astype(F32) * _dot(od_ref[...], wd_ref[...])
             + gate_ref[:, 2 * d:3 * d].astype(F32) * _dot(os_ref[...], ws_ref[...]))
    y_ref[...] = x_ref[...] + _dot(mixed.astype(BF16), wo_ref[...])


def _merge(x2d, slab2d, o_fox, o_diff, o_stick, w_bf, w_bd, w_bs, w_out):
    n_tok = x2d.shape[0]
    tm = TOKEN_TILE
    const = lambda i: (0, 0)
    rows = lambda i: (i, 0)
    resident = lambda w: pl.BlockSpec(w.shape, const, pipeline_mode=pl.Buffered(1))
    return pl.pallas_call(
        _merge_kernel,
        grid=(n_tok // tm,),
        in_specs=[
            pl.BlockSpec((tm, D_MODEL), rows),
            pl.BlockSpec((tm, N_BRANCH * D_MODEL), lambda i: (i, OFF_GATE // (N_BRANCH * D_MODEL))),
            pl.BlockSpec((tm, W_FOX), rows),
            pl.BlockSpec((tm, W_DIFF), rows),
            pl.BlockSpec((tm, W_STICK), rows),
            resident(w_bf), resident(w_bd), resident(w_bs), resident(w_out),
        ],
        out_specs=pl.BlockSpec((tm, D_MODEL), rows),
        out_shape=jax.ShapeDtypeStruct((n_tok, D_MODEL), F32),
        compiler_params=pltpu.CompilerParams(
            dimension_semantics=("arbitrary",), vmem_limit_bytes=VMEM_LIMIT_BYTES),
        name="merge_outproj",
    )(x2d, slab2d, o_fox, o_diff, o_stick, w_bf, w_bd, w_bs, w_out)


def _shift_rows(u, prev, n):
    rolled = pltpu.roll(u, n, 0)
    head = rolled[:SUBLANES]
    row = lax.broadcasted_iota(jnp.int32, head.shape, 0)
    for r in range(n):
        src = SUBLANES - n + r
        head = jnp.where(row == r, prev[src:src + 1], head)
    return jnp.concatenate([head, rolled[SUBLANES:]], axis=0)


def _ffn_kernel(tiles_per_seq, final, x_ref, g_ref, wup_ref, cw_ref, cb_ref, wdn_ref, gfin_ref,
                y_ref, hn_ref, act_ref, tail_ref):
    tm = x_ref.shape[0]

    @pl.when(pl.program_id(0) % tiles_per_seq == 0)
    def _():
        tail_ref[...] = jnp.zeros(tail_ref.shape, F32)

    x = x_ref[...]
    hn_ref[...] = ((x * _rms_scale(x)) * g_ref[...]).astype(BF16)
    for c in range(D_FF // FF_CHUNK):
        conv = []
        for part in range(2):
            cols = slice(part * D_FF + c * FF_CHUNK, part * D_FF + (c + 1) * FF_CHUNK)
            u = _dot(hn_ref[...], wup_ref[:, cols])
            prev = tail_ref[:, cols]
            tail_ref[:, cols] = u[tm - SUBLANES:]
            conv.append(cw_ref[0:1, cols] * _shift_rows(u, prev, 2)
                        + cw_ref[1:2, cols] * _shift_rows(u, prev, 1)
                        + cw_ref[2:3, cols] * u + cb_ref[:, cols])
        gate, val = conv
        act_ref[:, c * FF_CHUNK:(c + 1) * FF_CHUNK] = (gate * jax.nn.sigmoid(gate) * val).astype(BF16)
    y = x + _dot(act_ref[...], wdn_ref[...])
    if final:
        y = (y * _rms_scale(y)) * gfin_ref[...]
    y_ref[...] = y


def _ffn(x2d, g, w_up, conv_w, conv_b, w_down, g_final, seq_len, final):
    n_tok = x2d.shape[0]
    tm = TOKEN_TILE
    const = lambda i: (0, 0)
    rows = lambda i: (i, 0)
    resident = lambda w: pl.BlockSpec(w.shape, const, pipeline_mode=pl.Buffered(1))
    return pl.pallas_call(
        functools.partial(_ffn_kernel, seq_len // tm, final),
        grid=(n_tok // tm,),
        in_specs=[
            pl.BlockSpec((tm, D_MODEL), rows),
            pl.BlockSpec((1, D_MODEL), const),
            resident(w_up),
            pl.BlockSpec(conv_w.shape, const),
            pl.BlockSpec(conv_b.shape, const),
            resident(w_down),
            pl.BlockSpec((1, D_MODEL), const),
        ],
        out_specs=pl.BlockSpec((tm, D_MODEL), rows),
        out_shape=jax.ShapeDtypeStruct((n_tok, D_MODEL), F32),
        scratch_shapes=[
            pltpu.VMEM((tm, D_MODEL), BF16),
            pltpu.VMEM((tm, D_FF), BF16),
            pltpu.VMEM((SUBLANES, 2 * D_FF), F32),
        ],
        compiler_params=pltpu.CompilerParams(
            dimension_semantics=("arbitrary",), vmem_limit_bytes=VMEM_LIMIT_BYTES),
        name="conv_ffn",
    )(x2d, g, w_up, conv_w, conv_b, w_down, g_final)


def _rope_tables(seq_len):
    half = ROPE_DIM // 2
    pos = jnp.arange(seq_len, dtype=F32)
    inv_freq = ROPE_THETA ** (-jnp.arange(0, ROPE_DIM, 2, dtype=F32) / ROPE_DIM)
    ang = pos[:, None] * inv_freq[None, :]
    cos, sin = jnp.cos(ang), jnp.sin(ang)
    ones = jnp.ones((seq_len, HEAD_DIM - ROPE_DIM), F32)
    zeros = jnp.zeros((seq_len, HEAD_DIM - ROPE_DIM), F32)
    zh = jnp.zeros((seq_len, half), F32)
    cos_h = jnp.concatenate([cos, cos, ones], axis=-1)
    sup_h = jnp.concatenate([zh, sin, zeros], axis=-1)
    sdn_h = jnp.concatenate([-sin, zh, zeros], axis=-1)
    rep = LANES // HEAD_DIM
    return tuple(jnp.tile(t, (1, rep)) for t in (cos_h, sup_h, sdn_h))


def _slab_weight(w_in_l):
    n_f = OFF_GATE
    w = jnp.concatenate([w_in_l[:, :n_f], w_in_l[:, n_f + H_FOX:]], axis=1)
    scale = jnp.ones((N_SLAB,), F32)
    for off, width in ((OFF_FQ, W_FOX), (OFF_DQ, W_DIFF_QK), (OFF_SQ, W_STICK)):
        scale = scale.at[off:off + width].set(QK_SCALE)
    return (w * scale[None, :]).astype(BF16)


def kernel(x, attn_norm_g, w_in, forget_bias, lam_q1, lam_k1, lam_q2, lam_k2, diff_subln_g,
           w_br_fox, w_br_diff, w_br_stick, w_out, ffn_norm_g, w_up, conv_w, conv_b, w_down,
           final_norm_g):
    b, seq_len, d = x.shape
    depth = w_in.shape[0]
    n_tok = b * seq_len
    cos_t, sup_t, sdn_t = _rope_tables(seq_len)
    x2d = x.reshape(n_tok, d)
    g_final = final_norm_g.reshape(1, d)

    for l in range(depth):
        lam_init = 0.8 - 0.6 * math.exp(-0.3 * l)
        w_f = jnp.pad(w_in[l][:, OFF_GATE:OFF_GATE + H_FOX], ((0, 0), (0, LANES - H_FOX))).astype(BF16)
        f_bias = jnp.pad(forget_bias[l], (0, LANES - H_FOX)).reshape(1, LANES)
        slab2d, logf = _inproj(x2d, attn_norm_g[l].reshape(1, d), _slab_weight(w_in[l]), w_f, f_bias,
                               cos_t, sup_t, sdn_t, seq_len)
        slab = slab2d.reshape(b, seq_len, N_SLAB)

        logf_t = logf.reshape(b, seq_len, LANES)[:, :, :H_FOX].transpose(0, 2, 1)
        logf_t = logf_t.reshape(b, H_FOX // 2, 2, seq_len)
        logf_t = jnp.pad(logf_t, ((0, 0), (0, 0), (0, DECAY_ROWS - 2), (0, 0)))
        negc = _decay(logf_t)

        o_fox = _fox_attention(slab, negc)
        lam_params = jnp.stack([lam_q1[l], lam_k1[l], lam_q2[l], lam_k2[l]]).astype(F32)
        o_diff = _diff_attention(slab, lam_params, diff_subln_g[l].reshape(1, -1), lam_init)
        o_stick = _stick_attention(slab)

        x2d = _merge(x2d, slab2d, o_fox.reshape(n_tok, -1), o_diff.reshape(n_tok, -1),
                     o_stick.reshape(n_tok, -1), w_br_fox[l].astype(BF16), w_br_diff[l].astype(BF16),
                     w_br_stick[l].astype(BF16), w_out[l].astype(BF16))
        x2d = _ffn(x2d, ffn_norm_g[l].reshape(1, d), w_up[l].astype(BF16), conv_w[l],
                   conv_b[l].reshape(1, -1), w_down[l].astype(BF16), g_final,
                   seq_len, final=(l == depth - 1))
    return x2d.reshape(b, seq_len, d)
```

```python
import functools
import math

import jax
import jax.numpy as jnp
from jax import lax
from jax.experimental import pallas as pl
from jax.experimental.pallas import tpu as pltpu

F32 = jnp.float32
BF16 = jnp.bfloat16

D_MODEL = 1024
HEAD_DIM = 64
H_FOX = 4
H_DIFF = 4
H_STICK = 4
W_FOX = H_FOX * HEAD_DIM
W_DIFF_QK = 2 * H_DIFF * HEAD_DIM
W_DIFF = H_DIFF * 2 * HEAD_DIM
W_STICK = H_STICK * HEAD_DIM
N_BRANCH = 3
ROPE_THETA = 500000.0
ROPE_DIM = HEAD_DIM // 4
D_FF = 2816
CONV_WIDTH = 3
NORM_EPS = 1e-6
QK_SCALE = HEAD_DIM ** -0.5
LOG2_E = math.log2(math.e)

LANES = 128
SUBLANES = 8
VMEM_LIMIT_BYTES = 56 * 1024 * 1024

OFF_FQ = 0
OFF_FK = OFF_FQ + W_FOX
OFF_FV = OFF_FK + W_FOX
OFF_DQ = OFF_FV + W_FOX
OFF_DK = OFF_DQ + W_DIFF_QK
OFF_DV = OFF_DK + W_DIFF_QK
OFF_SQ = OFF_DV + W_DIFF
OFF_SK = OFF_SQ + W_STICK
OFF_SV = OFF_SK + W_STICK
OFF_GATE = OFF_SV + W_STICK
N_SLAB = OFF_GATE + N_BRANCH * D_MODEL

TOKEN_TILE = 512
ATTN_TILE = 256
Q_CHUNKS = 4
SOFTMAX_OFF_KEYS = 512
FF_CHUNK = 256


def _log_sigmoid_parts(z):
    l1p = jnp.log(1.0 + jnp.exp(-jnp.abs(z)))
    return jnp.minimum(z, 0.0) - l1p, -jnp.maximum(z, 0.0) - l1p


def _rms_scale(x):
    return lax.rsqrt(jnp.mean(x * x, axis=-1, keepdims=True) + NORM_EPS)


def _split_bf16(v, n_parts):
    parts = []
    rem = v
    for _ in range(n_parts):
        p = rem.astype(BF16)
        parts.append(p)
        rem = rem - p.astype(F32)
    return parts


def _dot_nt(a, b):
    return lax.dot_general(a, b, (((1,), (1,)), ((), ())), preferred_element_type=F32)


def _dot(a, b):
    return jnp.dot(a, b, preferred_element_type=F32)


_PROJ_CHUNKS = (
    [(OFF_FQ, 3 * W_FOX // 2, "plain"), (OFF_FQ + 3 * W_FOX // 2, 3 * W_FOX // 2, "plain")]
    + [(OFF_DQ, W_DIFF_QK, "rope"), (OFF_DK, W_DIFF_QK, "rope"), (OFF_DV, W_DIFF, "plain")]
    + [(OFF_SQ, 3 * W_STICK // 2, "plain"), (OFF_SQ + 3 * W_STICK // 2, 3 * W_STICK // 2, "plain")]
    + [(OFF_GATE + c * 512, 512, "sigmoid") for c in range(N_BRANCH * D_MODEL // 512)]
)


def _rope_lanes(t, cos, sin_up, sin_dn):
    return (t * cos + pltpu.roll(t, ROPE_DIM // 2, 1) * sin_up
            + pltpu.roll(t, LANES - ROPE_DIM // 2, 1) * sin_dn)


def _inproj_kernel(tiles_per_seq, x_ref, g_ref, w_ref, wf_ref, fb_ref, tri_ref, cos_ref, sup_ref,
                   sdn_ref, slab_ref, decay_ref, xn_ref, carry_ref):
    @pl.when(pl.program_id(0) % tiles_per_seq == 0)
    def _():
        carry_ref[...] = jnp.zeros(carry_ref.shape, F32)

    x = x_ref[...]
    xn_ref[...] = ((x * _rms_scale(x)) * g_ref[...]).astype(BF16)
    for off, width, kind in _PROJ_CHUNKS:
        res = _dot(xn_ref[...], w_ref[:, off:off + width])
        if kind == "rope":
            cos, sup, sdn = cos_ref[...], sup_ref[...], sdn_ref[...]
            for c in range(width // LANES):
                blk = _rope_lanes(res[:, c * LANES:(c + 1) * LANES], cos, sup, sdn)
                slab_ref[:, off + c * LANES:off + (c + 1) * LANES] = blk.astype(BF16)
        elif kind == "sigmoid":
            slab_ref[:, off:off + width] = jax.nn.sigmoid(res).astype(BF16)
        else:
            slab_ref[:, off:off + width] = res.astype(BF16)
    log_f = _log_sigmoid_parts(_dot(xn_ref[...], wf_ref[...]) + fb_ref[...])[0]
    tri = tri_ref[...]
    c = carry_ref[0:1, :]
    for part in _split_bf16(log_f, 3):
        c = c + _dot(tri, part)
    tm = c.shape[0]
    carry_ref[...] = jnp.broadcast_to(c[tm - 1:tm, :], carry_ref.shape)
    hi, mid, lo = _split_bf16(c * (-LOG2_E), 3)
    lane = lax.broadcasted_iota(jnp.int32, c.shape, 1)
    part = jnp.where(lane < H_FOX, hi, jnp.where(lane < 2 * H_FOX, mid, lo))
    decay_ref[...] = jnp.where(lane < 3 * H_FOX, part, jnp.zeros_like(part))


def _inproj(x2d, g, w_slab, w_f, f_bias, cos_t, sup_t, sdn_t, seq_len):
    n_tok = x2d.shape[0]
    tm = TOKEN_TILE
    tiles_per_seq = seq_len // tm
    const = lambda i: (0, 0)
    rows = lambda i: (i, 0)
    rope_map = lambda i: (i % tiles_per_seq, 0)
    tri = jnp.tril(jnp.ones((tm, tm), BF16))
    return pl.pallas_call(
        functools.partial(_inproj_kernel, tiles_per_seq),
        grid=(n_tok // tm,),
        in_specs=[
            pl.BlockSpec((tm, D_MODEL), rows),
            pl.BlockSpec((1, D_MODEL), const),
            pl.BlockSpec((D_MODEL, N_SLAB), const, pipeline_mode=pl.Buffered(1)),
            pl.BlockSpec((D_MODEL, LANES), const, pipeline_mode=pl.Buffered(1)),
            pl.BlockSpec((1, LANES), const),
            pl.BlockSpec((tm, tm), const, pipeline_mode=pl.Buffered(1)),
            pl.BlockSpec((tm, LANES), rope_map),
            pl.BlockSpec((tm, LANES), rope_map),
            pl.BlockSpec((tm, LANES), rope_map),
        ],
        out_specs=[
            pl.BlockSpec((tm, N_SLAB), rows),
            pl.BlockSpec((tm, LANES), rows),
        ],
        out_shape=[
            jax.ShapeDtypeStruct((n_tok, N_SLAB), BF16),
            jax.ShapeDtypeStruct((n_tok, LANES), BF16),
        ],
        scratch_shapes=[pltpu.VMEM((tm, D_MODEL), BF16), pltpu.VMEM((SUBLANES, LANES), F32)],
        compiler_params=pltpu.CompilerParams(
            dimension_semantics=("arbitrary",), vmem_limit_bytes=VMEM_LIMIT_BYTES),
        name="inproj",
    )(x2d, g, w_slab, w_f, f_bias, tri, cos_t, sup_t, sdn_t)


def _stack_query_halves(q_ref, qm_ref):
    t = ATTN_TILE
    lane = lax.broadcasted_iota(jnp.int32, (t, LANES), 1)
    for r in range(Q_CHUNKS):
        q = q_ref[0, r * t:(r + 1) * t, :]
        zero = jnp.zeros_like(q)
        qm_ref[(2 * r) * t:(2 * r + 1) * t, 0:LANES] = jnp.where(lane < HEAD_DIM, q, zero)
        qm_ref[(2 * r + 1) * t:(2 * r + 2) * t, 0:LANES] = jnp.where(lane >= HEAD_DIM, q, zero)


def _key_before_query(strict):
    query = lax.broadcasted_iota(jnp.int32, (ATTN_TILE, ATTN_TILE), 0)
    key = lax.broadcasted_iota(jnp.int32, (ATTN_TILE, ATTN_TILE), 1)
    return key < query if strict else key <= query


def _causal_key_tiles(step, off_keys, descending):
    t = ATTN_TILE
    block_start = Q_CHUNKS * t * pl.program_id(2)
    n_off = pl.program_id(2) * (Q_CHUNKS * t // off_keys)

    def off_diagonal(jj, carry):
        j = n_off - 1 - jj if descending else jj
        step(pl.ds(pl.multiple_of(j * off_keys, off_keys), off_keys), None)
        return carry

    def diagonal(d):
        step(pl.ds(pl.multiple_of(block_start + d * t, t), t), d)

    if descending:
        for d in reversed(range(Q_CHUNKS)):
            diagonal(d)
        lax.fori_loop(0, n_off, off_diagonal, 0)
    else:
        lax.fori_loop(0, n_off, off_diagonal, 0)
        for d in range(Q_CHUNKS):
            diagonal(d)


def _visible_chunks(diag_chunk):
    first = 0 if diag_chunk is None else diag_chunk
    return [(r, r == diag_chunk) for r in range(first, Q_CHUNKS)]


def _half_rows(r, half):
    return slice((2 * r + half) * ATTN_TILE, (2 * r + half + 1) * ATTN_TILE)


def _chunk_scores(qm_ref, k_tile, r):
    return _dot_nt(qm_ref[_half_rows(r, 0).start:_half_rows(r, 1).stop, :], k_tile)


def _softmax_update(s, on_diagonal, v_aug, rows, m_ref, acc_ref):
    if on_diagonal:
        s = jnp.where(_key_before_query(strict=False), s, -jnp.inf)
    m_prev = m_ref[rows, :]
    m_new = jnp.maximum(m_prev, jnp.max(s, axis=-1, keepdims=True))
    p = jnp.exp2(s - jnp.concatenate([m_new] * (s.shape[1] // LANES), axis=1)).astype(BF16)
    alpha = jnp.exp2(m_prev - m_new)
    if acc_ref.shape[1] != LANES:
        alpha = jnp.concatenate([alpha] * (acc_ref.shape[1] // LANES), axis=1)
    acc_ref[rows, :] = alpha * acc_ref[rows, :] + _dot(p, v_aug)
    m_ref[rows, :] = m_new


def _init_softmax_state(m_ref, acc_ref):
    m_ref[...] = jnp.full(m_ref.shape, -jnp.inf, F32)
    acc_ref[...] = jnp.zeros(acc_ref.shape, F32)


def _fox_kernel(q_ref, k_ref, v_ref, decay_ref, o_ref, qm_ref, m_ref, acc_ref):
    t = ATTN_TILE
    _stack_query_halves(q_ref, qm_ref)
    _init_softmax_state(m_ref, acc_ref)
    lane = lax.broadcasted_iota(jnp.int32, (t, LANES), 1)
    for half in range(2):
        head = 2 * pl.program_id(1) + half
        own = (lane == head) | (lane == head + H_FOX) | (lane == head + 2 * H_FOX)
        ones_in_own_lanes = jnp.where(own, 1.0, 0.0).astype(BF16)
        for r in range(Q_CHUNKS):
            qm_ref[_half_rows(r, half), LANES:2 * LANES] = ones_in_own_lanes

    def step(ks, diag_chunk):
        k_aug = jnp.concatenate([k_ref[0, ks, :], decay_ref[0, ks, :]], axis=1)
        v = v_ref[0, ks, :]
        one = jnp.ones_like(v)
        lane_k = lax.broadcasted_iota(jnp.int32, v.shape, 1)
        v_aug = [jnp.where(lane_k < HEAD_DIM, v, one), jnp.where(lane_k >= HEAD_DIM, v, one)]
        for r, on_diagonal in _visible_chunks(diag_chunk):
            s = _chunk_scores(qm_ref, k_aug, r)
            for half in range(2):
                _softmax_update(s[half * t:(half + 1) * t], on_diagonal, v_aug[half],
                                _half_rows(r, half), m_ref, acc_ref)

    _causal_key_tiles(step, SOFTMAX_OFF_KEYS, descending=False)
    for r in range(Q_CHUNKS):
        a0 = acc_ref[(2 * r) * t:(2 * r + 1) * t, :]
        a1 = acc_ref[(2 * r + 1) * t:(2 * r + 2) * t, :]
        num = jnp.where(lane < HEAD_DIM, a0, a1)
        den = jnp.where(lane < HEAD_DIM, pltpu.roll(a0, HEAD_DIM, 1), pltpu.roll(a1, HEAD_DIM, 1))
        o_ref[0, r * t:(r + 1) * t, :] = (num / den).astype(o_ref.dtype)


def _diff_kernel(lam_init, q_ref, k_ref, v_ref, lam_ref, g_ref, o_ref, qm_ref, m_ref, acc_ref):
    t = ATTN_TILE
    dv = 2 * HEAD_DIM
    _stack_query_halves(q_ref, qm_ref)
    _init_softmax_state(m_ref, acc_ref)

    def step(ks, diag_chunk):
        k_tile = k_ref[0, ks, :]
        v = v_ref[0, ks, :]
        v_aug = jnp.concatenate([v, jnp.ones_like(v)], axis=1)
        for r, on_diagonal in _visible_chunks(diag_chunk):
            s = _chunk_scores(qm_ref, k_tile, r)
            for half in range(2):
                _softmax_update(s[half * t:(half + 1) * t], on_diagonal, v_aug,
                                _half_rows(r, half), m_ref, acc_ref)

    _causal_key_tiles(step, SOFTMAX_OFF_KEYS, descending=False)
    lp = lam_ref[...]
    lam = (jnp.exp(jnp.sum(lp[0:1] * lp[1:2], axis=-1, keepdims=True))
           - jnp.exp(jnp.sum(lp[2:3] * lp[3:4], axis=-1, keepdims=True)) + lam_init)
    for r in range(Q_CHUNKS):
        a1 = acc_ref[(2 * r) * t:(2 * r + 1) * t, :]
        a2 = acc_ref[(2 * r + 1) * t:(2 * r + 2) * t, :]
        o = a1[:, :dv] / a1[:, dv:] - lam * (a2[:, :dv] / a2[:, dv:])
        o = (o * _rms_scale(o)) * g_ref[...]
        o_ref[0, r * t:(r + 1) * t, :] = (o * (1.0 - lam_init)).astype(o_ref.dtype)


def _stick_kernel(q_ref, k_ref, v_ref, o_ref, qm_ref, suffix_ref, r_ref, acc_ref):
    t = ATTN_TILE
    _stack_query_halves(q_ref, qm_ref)
    r_ref[...] = jnp.zeros(r_ref.shape, F32)
    acc_ref[...] = jnp.zeros(acc_ref.shape, F32)
    row = lax.broadcasted_iota(jnp.int32, (t, t), 0)
    col = lax.broadcasted_iota(jnp.int32, (t, t), 1)
    suffix_ref[...] = jnp.where(row >= col, -1.0, 0.0).astype(BF16)

    def step(ks, diag_chunk):
        k_tile = k_ref[0, ks, :]
        v = v_ref[0, ks, :]
        suffix = suffix_ref[...]
        for r, on_diagonal in _visible_chunks(diag_chunk):
            z_all = _chunk_scores(qm_ref, k_tile, r)
            for half in range(2):
                rows = _half_rows(r, half)
                z = z_all[half * t:(half + 1) * t]
                neg_log_1m = jnp.maximum(z, 0.0) + jnp.log2(1.0 + jnp.exp2(-jnp.abs(z)))
                if on_diagonal:
                    neg_log_1m = jnp.where(_key_before_query(strict=True), neg_log_1m, 0.0)
                hi, lo = _split_bf16(neg_log_1m, 2)
                sums = _dot(hi, suffix) + _dot(lo, suffix)
                r_prev = r_ref[rows, :]
                w = jnp.exp2(z + sums + jnp.concatenate([r_prev] * (t // LANES), axis=1))
                if on_diagonal:
                    w = jnp.where(_key_before_query(strict=True), w, 0.0)
                acc_ref[rows, :] += _dot(w.astype(BF16), v)
                r_ref[rows, :] = r_prev + jnp.broadcast_to(sums[:, 0:1], (t, LANES))

    _causal_key_tiles(step, ATTN_TILE, descending=True)
    lane = lax.broadcasted_iota(jnp.int32, (t, LANES), 1)
    for r in range(Q_CHUNKS):
        a0 = acc_ref[(2 * r) * t:(2 * r + 1) * t, :]
        a1 = acc_ref[(2 * r + 1) * t:(2 * r + 2) * t, :]
        o_ref[0, r * t:(r + 1) * t, :] = jnp.where(lane < HEAD_DIM, a0, a1).astype(o_ref.dtype)


def _attention_call(kernel, slab, extra_inputs, extra_specs, n_blocks, q_blk, k_blk, v_blk,
                    scratch, name, query_lanes=LANES):
    b, seq_len, _ = slab.shape
    t = ATTN_TILE
    tq = Q_CHUNKS * t
    q_spec = pl.BlockSpec((1, tq, LANES), lambda bi, h, i: (bi, i, q_blk + h))
    k_spec = pl.BlockSpec((1, seq_len, LANES), lambda bi, h, i: (bi, 0, k_blk + h))
    v_spec = pl.BlockSpec((1, seq_len, LANES), lambda bi, h, i: (bi, 0, v_blk + h))
    return pl.pallas_call(
        kernel,
        grid=(b, n_blocks, seq_len // tq),
        in_specs=[q_spec, k_spec, v_spec] + extra_specs,
        out_specs=pl.BlockSpec((1, tq, LANES), lambda bi, h, i: (bi, i, h)),
        out_shape=jax.ShapeDtypeStruct((b, seq_len, n_blocks * LANES), BF16),
        scratch_shapes=[pltpu.VMEM((2 * tq, query_lanes), BF16)] + scratch,
        compiler_params=pltpu.CompilerParams(
            dimension_semantics=("arbitrary", "arbitrary", "arbitrary"),
            vmem_limit_bytes=VMEM_LIMIT_BYTES),
        name=name,
    )(slab, slab, slab, *extra_inputs)


def _softmax_scratch(acc_lanes):
    rows = 2 * Q_CHUNKS * ATTN_TILE
    return [pltpu.VMEM((rows, LANES), F32), pltpu.VMEM((rows, acc_lanes), F32)]


def _fox_attention(slab, decay):
    decay_spec = pl.BlockSpec((1,) + decay.shape[1:], lambda bi, h, i: (bi, 0, 0))
    return _attention_call(
        _fox_kernel, slab, [decay], [decay_spec], H_FOX // 2,
        OFF_FQ // LANES, OFF_FK // LANES, OFF_FV // LANES, _softmax_scratch(LANES), "fox_attention",
        query_lanes=2 * LANES)


def _diff_attention(slab, lam_params, subln_g, lam_init):
    const = lambda bi, h, i: (0, 0)
    specs = [pl.BlockSpec(lam_params.shape, const), pl.BlockSpec(subln_g.shape, const)]
    return _attention_call(
        functools.partial(_diff_kernel, lam_init), slab, [lam_params, subln_g], specs, H_DIFF,
        OFF_DQ // LANES, OFF_DK // LANES, OFF_DV // LANES, _softmax_scratch(2 * LANES),
        "diff_attention")


def _stick_attention(slab):
    t = ATTN_TILE
    rows = 2 * Q_CHUNKS * t
    scratch = [pltpu.VMEM((t, t), BF16), pltpu.VMEM((rows, LANES), F32),
               pltpu.VMEM((rows, LANES), F32)]
    return _attention_call(
        _stick_kernel, slab, [], [], H_STICK // 2,
        OFF_SQ // LANES, OFF_SK // LANES, OFF_SV // LANES, scratch, "stick_attention")


def _merge_kernel(x_ref, gate_ref, of_ref, od_ref, os_ref, wf_ref, wd_ref, ws_ref, wo_ref, y_ref):
    d = D_MODEL
    mixed = (gate_ref[:, 0:d].astype(F32) * _dot(of_ref[...], wf_ref[...])
             + gate_ref[:, d:2 * d].astype(F32) * _dot(od_ref[...], wd_ref[...])
             + gate_ref[:, 2 * d:3 * d].astype(F32) * _dot(os_ref[...], ws_ref[...]))
    y_ref[...] = x_ref[...] + _dot(mixed.astype(BF16), wo_ref[...])


def _merge(x2d, slab2d, o_fox, o_diff, o_stick, w_bf, w_bd, w_bs, w_out):
    n_tok = x2d.shape[0]
    tm = TOKEN_TILE
    const = lambda i: (0, 0)
    rows = lambda i: (i, 0)
    resident = lambda w: pl.BlockSpec(w.shape, const, pipeline_mode=pl.Buffered(1))
    return pl.pallas_call(
        _merge_kernel,
        grid=(n_tok // tm,),
        in_specs=[
            pl.BlockSpec((tm, D_MODEL), rows),
            pl.BlockSpec((tm, N_BRANCH * D_MODEL), lambda i: (i, OFF_GATE // (N_BRANCH * D_MODEL))),
            pl.BlockSpec((tm, W_FOX), rows),
            pl.BlockSpec((tm, W_DIFF), rows),
            pl.BlockSpec((tm, W_STICK), rows),
            resident(w_bf), resident(w_bd), resident(w_bs), resident(w_out),
        ],
        out_specs=pl.BlockSpec((tm, D_MODEL), rows),
        out_shape=jax.ShapeDtypeStruct((n_tok, D_MODEL), F32),
        compiler_params=pltpu.CompilerParams(
            dimension_semantics=("arbitrary",), vmem_limit_bytes=VMEM_LIMIT_BYTES),
        name="merge_outproj",
    )(x2d, slab2d, o_fox, o_diff, o_stick, w_bf, w_bd, w_bs, w_out)


def _shift_rows(u, prev, n):
    rolled = pltpu.roll(u, n, 0)
    head = rolled[:SUBLANES]
    row = lax.broadcasted_iota(jnp.int32, head.shape, 0)
    for r in range(n):
        src = SUBLANES - n + r
        head = jnp.where(row == r, prev[src:src + 1], head)
    return jnp.concatenate([head, rolled[SUBLANES:]], axis=0)


def _ffn_kernel(tiles_per_seq, final, x_ref, g_ref, wup_ref, cw_ref, cb_ref, wdn_ref, gfin_ref,
                y_ref, hn_ref, act_ref, tail_ref):
    tm = x_ref.shape[0]

    @pl.when(pl.program_id(0) % tiles_per_seq == 0)
    def _():
        tail_ref[...] = jnp.zeros(tail_ref.shape, F32)

    x = x_ref[...]
    hn_ref[...] = ((x * _rms_scale(x)) * g_ref[...]).astype(BF16)
    for c in range(D_FF // FF_CHUNK):
        conv = []
        for part in range(2):
            cols = slice(part * D_FF + c * FF_CHUNK, part * D_FF + (c + 1) * FF_CHUNK)
            u = _dot(hn_ref[...], wup_ref[:, cols])
            prev = tail_ref[:, cols]
            tail_ref[:, cols] = u[tm - SUBLANES:]
            conv.append(cw_ref[0:1, cols] * _shift_rows(u, prev, 2)
                        + cw_ref[1:2, cols] * _shift_rows(u, prev, 1)
                        + cw_ref[2:3, cols] * u + cb_ref[:, cols])
        gate, val = conv
        act_ref[:, c * FF_CHUNK:(c + 1) * FF_CHUNK] = (gate * jax.nn.sigmoid(gate) * val).astype(BF16)
    y = x + _dot(act_ref[...], wdn_ref[...])
    if final:
        y = (y * _rms_scale(y)) * gfin_ref[...]
    y_ref[...] = y


def _ffn(x2d, g, w_up, conv_w, conv_b, w_down, g_final, seq_len, final):
    n_tok = x2d.shape[0]
    tm = TOKEN_TILE
    const = lambda i: (0, 0)
    rows = lambda i: (i, 0)
    resident = lambda w: pl.BlockSpec(w.shape, const, pipeline_mode=pl.Buffered(1))
    return pl.pallas_call(
        functools.partial(_ffn_kernel, seq_len // tm, final),
        grid=(n_tok // tm,),
        in_specs=[
            pl.BlockSpec((tm, D_MODEL), rows),
            pl.BlockSpec((1, D_MODEL), const),
            resident(w_up),
            pl.BlockSpec(conv_w.shape, const),
            pl.BlockSpec(conv_b.shape, const),
            resident(w_down),
            pl.BlockSpec((1, D_MODEL), const),
        ],
        out_specs=pl.BlockSpec((tm, D_MODEL), rows),
        out_shape=jax.ShapeDtypeStruct((n_tok, D_MODEL), F32),
        scratch_shapes=[
            pltpu.VMEM((tm, D_MODEL), BF16),
            pltpu.VMEM((tm, D_FF), BF16),
            pltpu.VMEM((SUBLANES, 2 * D_FF), F32),
        ],
        compiler_params=pltpu.CompilerParams(
            dimension_semantics=("arbitrary",), vmem_limit_bytes=VMEM_LIMIT_BYTES),
        name="conv_ffn",
    )(x2d, g, w_up, conv_w, conv_b, w_down, g_final)


def _rope_tables(seq_len):
    half = ROPE_DIM // 2
    pos = jnp.arange(seq_len, dtype=F32)
    inv_freq = ROPE_THETA ** (-jnp.arange(0, ROPE_DIM, 2, dtype=F32) / ROPE_DIM)
    ang = pos[:, None] * inv_freq[None, :]
    cos, sin = jnp.cos(ang), jnp.sin(ang)
    ones = jnp.ones((seq_len, HEAD_DIM - ROPE_DIM), F32)
    zeros = jnp.zeros((seq_len, HEAD_DIM - ROPE_DIM), F32)
    zh = jnp.zeros((seq_len, half), F32)
    cos_h = jnp.concatenate([cos, cos, ones], axis=-1)
    sup_h = jnp.concatenate([zh, sin, zeros], axis=-1)
    sdn_h = jnp.concatenate([-sin, zh, zeros], axis=-1)
    rep = LANES // HEAD_DIM
    return tuple(jnp.tile(t, (1, rep)) for t in (cos_h, sup_h, sdn_h))


def _slab_weight(w_in_l):
    n_f = OFF_GATE
    w = jnp.concatenate([w_in_l[:, :n_f], w_in_l[:, n_f + H_FOX:]], axis=1)
    scale = jnp.ones((N_SLAB,), F32)
    for off, width in ((OFF_FQ, W_FOX), (OFF_DQ, W_DIFF_QK), (OFF_SQ, W_STICK)):
        scale = scale.at[off:off + width].set(QK_SCALE * LOG2_E)
    return (w * scale[None, :]).astype(BF16)


def _forget_weight(w_in_l, forget_bias_l):
    w_f = jnp.tile(w_in_l[:, OFF_GATE:OFF_GATE + H_FOX], (1, 3))
    w_f = jnp.pad(w_f, ((0, 0), (0, LANES - 3 * H_FOX))).astype(BF16)
    bias = jnp.pad(jnp.tile(forget_bias_l.astype(F32), 3), (0, LANES - 3 * H_FOX)).reshape(1, LANES)
    return w_f, bias


def kernel(x, attn_norm_g, w_in, forget_bias, lam_q1, lam_k1, lam_q2, lam_k2, diff_subln_g,
           w_br_fox, w_br_diff, w_br_stick, w_out, ffn_norm_g, w_up, conv_w, conv_b, w_down,
           final_norm_g):
    b, seq_len, d = x.shape
    depth = w_in.shape[0]
    n_tok = b * seq_len
    cos_t, sup_t, sdn_t = _rope_tables(seq_len)
    x2d = x.reshape(n_tok, d)
    g_final = final_norm_g.reshape(1, d)

    for l in range(depth):
        lam_init = 0.8 - 0.6 * math.exp(-0.3 * l)
        w_f, f_bias = _forget_weight(w_in[l], forget_bias[l])
        slab2d, decay = _inproj(x2d, attn_norm_g[l].reshape(1, d), _slab_weight(w_in[l]), w_f, f_bias,
                                cos_t, sup_t, sdn_t, seq_len)
        slab = slab2d.reshape(b, seq_len, N_SLAB)

        o_fox = _fox_attention(slab, decay.reshape(b, seq_len, LANES))
        lam_params = jnp.stack([lam_q1[l], lam_k1[l], lam_q2[l], lam_k2[l]]).astype(F32)
        o_diff = _diff_attention(slab, lam_params, diff_subln_g[l].reshape(1, -1), lam_init)
        o_stick = _stick_attention(slab)

        x2d = _merge(x2d, slab2d, o_fox.reshape(n_tok, -1), o_diff.reshape(n_tok, -1),
                     o_stick.reshape(n_tok, -1), w_br_fox[l].astype(BF16), w_br_diff[l].astype(BF16),
                     w_br_stick[l].astype(BF16), w_out[l].astype(BF16))
        x2d = _ffn(x2d, ffn_norm_g[l].reshape(1, d), w_up[l].astype(BF16), conv_w[l],
                   conv_b[l].reshape(1, -1), w_down[l].astype(BF16), g_final,
                   seq_len, final=(l == depth - 1))
    return x2d.reshape(b, seq_len, d)
```

```python
import functools
import math

import jax
import jax.numpy as jnp
from jax import lax
from jax.experimental import pallas as pl
from jax.experimental.pallas import tpu as pltpu

F32 = jnp.float32
BF16 = jnp.bfloat16

D_MODEL = 1024
HEAD_DIM = 64
H_FOX = 4
H_DIFF = 4
H_STICK = 4
W_FOX = H_FOX * HEAD_DIM
W_DIFF_QK = 2 * H_DIFF * HEAD_DIM
W_DIFF = H_DIFF * 2 * HEAD_DIM
W_STICK = H_STICK * HEAD_DIM
N_BRANCH = 3
ROPE_THETA = 500000.0
ROPE_DIM = HEAD_DIM // 4
D_FF = 2816
CONV_WIDTH = 3
NORM_EPS = 1e-6
QK_SCALE = HEAD_DIM ** -0.5
LOG2_E = math.log2(math.e)

LANES = 128
SUBLANES = 8
VMEM_LIMIT_BYTES = 56 * 1024 * 1024

OFF_FQ = 0
OFF_FK = OFF_FQ + W_FOX
OFF_FV = OFF_FK + W_FOX
OFF_DQ = OFF_FV + W_FOX
OFF_DK = OFF_DQ + W_DIFF_QK
OFF_DV = OFF_DK + W_DIFF_QK
OFF_SQ = OFF_DV + W_DIFF
OFF_SK = OFF_SQ + W_STICK
OFF_SV = OFF_SK + W_STICK
N_SLAB = OFF_SV + W_STICK
IN_OFF_FORGET = N_SLAB
IN_OFF_GATE = IN_OFF_FORGET + H_FOX

TOKEN_TILE = 512
ATTN_TILE = 256
Q_CHUNKS = 4
SOFTMAX_OFF_KEYS = 512
FF_CHUNK = 256


def _log_sigmoid_parts(z):
    l1p = jnp.log(1.0 + jnp.exp(-jnp.abs(z)))
    return jnp.minimum(z, 0.0) - l1p, -jnp.maximum(z, 0.0) - l1p


def _rms_scale(x):
    return lax.rsqrt(jnp.mean(x * x, axis=-1, keepdims=True) + NORM_EPS)


def _split_bf16(v, n_parts):
    parts = []
    rem = v
    for _ in range(n_parts):
        p = rem.astype(BF16)
        parts.append(p)
        rem = rem - p.astype(F32)
    return parts


def _dot_nt(a, b):
    return lax.dot_general(a, b, (((1,), (1,)), ((), ())), preferred_element_type=F32)


def _dot(a, b):
    return jnp.dot(a, b, preferred_element_type=F32)


_PROJ_CHUNKS = (
    [(OFF_FQ, 3 * W_FOX // 2, "plain"), (OFF_FQ + 3 * W_FOX // 2, 3 * W_FOX // 2, "plain")]
    + [(OFF_DQ, W_DIFF_QK, "rope"), (OFF_DK, W_DIFF_QK, "rope"), (OFF_DV, W_DIFF, "plain")]
    + [(OFF_SQ, 3 * W_STICK // 2, "plain"), (OFF_SQ + 3 * W_STICK // 2, 3 * W_STICK // 2, "plain")]
)


def _rope_lanes(t, cos, sin_up, sin_dn):
    return (t * cos + pltpu.roll(t, ROPE_DIM // 2, 1) * sin_up
            + pltpu.roll(t, LANES - ROPE_DIM // 2, 1) * sin_dn)


def _inproj_kernel(tiles_per_seq, x_ref, g_ref, w_ref, wf_ref, fb_ref, tri_ref, cos_ref, sup_ref,
                   sdn_ref, slab_ref, decay_ref, xn_ref, carry_ref):
    @pl.when(pl.program_id(0) % tiles_per_seq == 0)
    def _():
        carry_ref[...] = jnp.zeros(carry_ref.shape, F32)

    x = x_ref[...]
    xn_ref[...] = ((x * _rms_scale(x)) * g_ref[...]).astype(BF16)
    for off, width, kind in _PROJ_CHUNKS:
        res = _dot(xn_ref[...], w_ref[:, off:off + width])
        if kind == "rope":
            cos, sup, sdn = cos_ref[...], sup_ref[...], sdn_ref[...]
            for c in range(width // LANES):
                blk = _rope_lanes(res[:, c * LANES:(c + 1) * LANES], cos, sup, sdn)
                slab_ref[:, off + c * LANES:off + (c + 1) * LANES] = blk.astype(BF16)
        else:
            slab_ref[:, off:off + width] = res.astype(BF16)
    log_f = _log_sigmoid_parts(_dot(xn_ref[...], wf_ref[...]) + fb_ref[...])[0]
    tri = tri_ref[...]
    c = carry_ref[0:1, :]
    for part in _split_bf16(log_f, 3):
        c = c + _dot(tri, part)
    tm = c.shape[0]
    carry_ref[...] = jnp.broadcast_to(c[tm - 1:tm, :], carry_ref.shape)
    hi, mid, lo = _split_bf16(c * (-LOG2_E), 3)
    lane = lax.broadcasted_iota(jnp.int32, c.shape, 1)
    part = jnp.where(lane < H_FOX, hi, jnp.where(lane < 2 * H_FOX, mid, lo))
    decay_ref[...] = jnp.where(lane < 3 * H_FOX, part, jnp.zeros_like(part))


def _inproj(x2d, g, w_slab, w_f, f_bias, cos_t, sup_t, sdn_t, seq_len):
    n_tok = x2d.shape[0]
    tm = TOKEN_TILE
    tiles_per_seq = seq_len // tm
    const = lambda i: (0, 0)
    rows = lambda i: (i, 0)
    rope_map = lambda i: (i % tiles_per_seq, 0)
    tri = jnp.tril(jnp.ones((tm, tm), BF16))
    return pl.pallas_call(
        functools.partial(_inproj_kernel, tiles_per_seq),
        grid=(n_tok // tm,),
        in_specs=[
            pl.BlockSpec((tm, D_MODEL), rows),
            pl.BlockSpec((1, D_MODEL), const),
            pl.BlockSpec((D_MODEL, N_SLAB), const, pipeline_mode=pl.Buffered(1)),
            pl.BlockSpec((D_MODEL, LANES), const, pipeline_mode=pl.Buffered(1)),
            pl.BlockSpec((1, LANES), const),
            pl.BlockSpec((tm, tm), const, pipeline_mode=pl.Buffered(1)),
            pl.BlockSpec((tm, LANES), rope_map),
            pl.BlockSpec((tm, LANES), rope_map),
            pl.BlockSpec((tm, LANES), rope_map),
        ],
        out_specs=[
            pl.BlockSpec((tm, N_SLAB), rows),
            pl.BlockSpec((tm, LANES), rows),
        ],
        out_shape=[
            jax.ShapeDtypeStruct((n_tok, N_SLAB), BF16),
            jax.ShapeDtypeStruct((n_tok, LANES), BF16),
        ],
        scratch_shapes=[pltpu.VMEM((tm, D_MODEL), BF16), pltpu.VMEM((SUBLANES, LANES), F32)],
        compiler_params=pltpu.CompilerParams(
            dimension_semantics=("arbitrary",), vmem_limit_bytes=VMEM_LIMIT_BYTES),
        name="inproj",
    )(x2d, g, w_slab, w_f, f_bias, tri, cos_t, sup_t, sdn_t)


def _stack_query_halves(q_ref, qm_ref):
    t = ATTN_TILE
    lane = lax.broadcasted_iota(jnp.int32, (t, LANES), 1)
    for r in range(Q_CHUNKS):
        q = q_ref[0, r * t:(r + 1) * t, :]
        zero = jnp.zeros_like(q)
        qm_ref[(2 * r) * t:(2 * r + 1) * t, 0:LANES] = jnp.where(lane < HEAD_DIM, q, zero)
        qm_ref[(2 * r + 1) * t:(2 * r + 2) * t, 0:LANES] = jnp.where(lane >= HEAD_DIM, q, zero)


def _key_before_query(strict):
    query = lax.broadcasted_iota(jnp.int32, (ATTN_TILE, ATTN_TILE), 0)
    key = lax.broadcasted_iota(jnp.int32, (ATTN_TILE, ATTN_TILE), 1)
    return key < query if strict else key <= query


def _causal_key_tiles(scores_fn, update_fn, off_keys, descending):
    t = ATTN_TILE
    block_start = Q_CHUNKS * t * pl.program_id(2)
    n_off = pl.program_id(2) * (Q_CHUNKS * t // off_keys)

    def step(ks, chunks):
        update_fn(ks, chunks, scores_fn(ks, chunks))

    def off_diagonal(jj, carry):
        j = n_off - 1 - jj if descending else jj
        step(pl.ds(pl.multiple_of(j * off_keys, off_keys), off_keys), _visible_chunks(None))
        return carry

    def diagonal(d):
        step(pl.ds(pl.multiple_of(block_start + d * t, t), t), _visible_chunks(d))

    if descending:
        for d in reversed(range(Q_CHUNKS)):
            diagonal(d)
        lax.fori_loop(0, n_off, off_diagonal, 0)
    else:
        lax.fori_loop(0, n_off, off_diagonal, 0)
        for d in range(Q_CHUNKS):
            diagonal(d)


def _visible_chunks(diag_chunk):
    first = 0 if diag_chunk is None else diag_chunk
    return [(r, r == diag_chunk) for r in range(first, Q_CHUNKS)]


def _half_rows(r, half):
    return slice((2 * r + half) * ATTN_TILE, (2 * r + half + 1) * ATTN_TILE)


def _chunk_scores(qm_ref, k_tile, r):
    return _dot_nt(qm_ref[_half_rows(r, 0).start:_half_rows(r, 1).stop, :], k_tile)


def _softmax_update(s, on_diagonal, v_aug, rows, m_ref, acc_ref):
    if on_diagonal:
        s = jnp.where(_key_before_query(strict=False), s, -jnp.inf)
    m_prev = m_ref[rows, :]
    m_new = jnp.maximum(m_prev, jnp.max(s, axis=-1, keepdims=True))
    p = jnp.exp2(s - jnp.concatenate([m_new] * (s.shape[1] // LANES), axis=1)).astype(BF16)
    alpha = jnp.exp2(m_prev - m_new)
    if acc_ref.shape[1] != LANES:
        alpha = jnp.concatenate([alpha] * (acc_ref.shape[1] // LANES), axis=1)
    acc_ref[rows, :] = alpha * acc_ref[rows, :] + _dot(p, v_aug)
    m_ref[rows, :] = m_new


def _init_softmax_state(m_ref, acc_ref):
    m_ref[...] = jnp.full(m_ref.shape, -jnp.inf, F32)
    acc_ref[...] = jnp.zeros(acc_ref.shape, F32)


def _fox_kernel(q_ref, k_ref, v_ref, decay_ref, o_ref, qm_ref, m_ref, acc_ref):
    t = ATTN_TILE
    _stack_query_halves(q_ref, qm_ref)
    _init_softmax_state(m_ref, acc_ref)
    lane = lax.broadcasted_iota(jnp.int32, (t, LANES), 1)
    for half in range(2):
        head = 2 * pl.program_id(1) + half
        own = (lane == head) | (lane == head + H_FOX) | (lane == head + 2 * H_FOX)
        ones_in_own_lanes = jnp.where(own, 1.0, 0.0).astype(BF16)
        for r in range(Q_CHUNKS):
            qm_ref[_half_rows(r, half), LANES:2 * LANES] = ones_in_own_lanes

    def scores(ks, chunks):
        k_aug = jnp.concatenate([k_ref[0, ks, :], decay_ref[0, ks, :]], axis=1)
        return [_chunk_scores(qm_ref, k_aug, r) for r, _ in chunks]

    def update(ks, chunks, score_list):
        v = v_ref[0, ks, :]
        one = jnp.ones_like(v)
        lane_k = lax.broadcasted_iota(jnp.int32, v.shape, 1)
        v_aug = [jnp.where(lane_k < HEAD_DIM, v, one), jnp.where(lane_k >= HEAD_DIM, v, one)]
        for (r, on_diagonal), s in zip(chunks, score_list):
            for half in range(2):
                _softmax_update(s[half * t:(half + 1) * t], on_diagonal, v_aug[half],
                                _half_rows(r, half), m_ref, acc_ref)

    _causal_key_tiles(scores, update, SOFTMAX_OFF_KEYS, descending=False)
    for r in range(Q_CHUNKS):
        a0 = acc_ref[(2 * r) * t:(2 * r + 1) * t, :]
        a1 = acc_ref[(2 * r + 1) * t:(2 * r + 2) * t, :]
        num = jnp.where(lane < HEAD_DIM, a0, a1)
        den = jnp.where(lane < HEAD_DIM, pltpu.roll(a0, HEAD_DIM, 1), pltpu.roll(a1, HEAD_DIM, 1))
        o_ref[0, r * t:(r + 1) * t, :] = (num / den).astype(o_ref.dtype)


def _diff_kernel(lam_init, q_ref, k_ref, v_ref, lam_ref, g_ref, o_ref, qm_ref, m_ref, acc_ref):
    t = ATTN_TILE
    dv = 2 * HEAD_DIM
    _stack_query_halves(q_ref, qm_ref)
    _init_softmax_state(m_ref, acc_ref)

    def scores(ks, chunks):
        k_tile = k_ref[0, ks, :]
        return [_chunk_scores(qm_ref, k_tile, r) for r, _ in chunks]

    def update(ks, chunks, score_list):
        v = v_ref[0, ks, :]
        v_aug = jnp.concatenate([v, jnp.ones_like(v)], axis=1)
        for (r, on_diagonal), s in zip(chunks, score_list):
            for half in range(2):
                _softmax_update(s[half * t:(half + 1) * t], on_diagonal, v_aug,
                                _half_rows(r, half), m_ref, acc_ref)

    _causal_key_tiles(scores, update, SOFTMAX_OFF_KEYS, descending=False)
    lp = lam_ref[...]
    lam = (jnp.exp(jnp.sum(lp[0:1] * lp[1:2], axis=-1, keepdims=True))
           - jnp.exp(jnp.sum(lp[2:3] * lp[3:4], axis=-1, keepdims=True)) + lam_init)
    for r in range(Q_CHUNKS):
        a1 = acc_ref[(2 * r) * t:(2 * r + 1) * t, :]
        a2 = acc_ref[(2 * r + 1) * t:(2 * r + 2) * t, :]
        o = a1[:, :dv] / a1[:, dv:] - lam * (a2[:, :dv] / a2[:, dv:])
        o = (o * _rms_scale(o)) * g_ref[...]
        o_ref[0, r * t:(r + 1) * t, :] = (o * (1.0 - lam_init)).astype(o_ref.dtype)


def _stick_kernel(q_ref, k_ref, v_ref, o_ref, qm_ref, suffix_ref, r_ref, acc_ref):
    t = ATTN_TILE
    _stack_query_halves(q_ref, qm_ref)
    r_ref[...] = jnp.zeros(r_ref.shape, F32)
    acc_ref[...] = jnp.zeros(acc_ref.shape, F32)
    row = lax.broadcasted_iota(jnp.int32, (2 * t, t), 0)
    col = lax.broadcasted_iota(jnp.int32, (2 * t, t), 1)
    suffix_ref[...] = jnp.where((row >= col) & ((row < t) | (row >= col + t)), -1.0, 0.0).astype(BF16)

    def scores(ks, chunks):
        k_tile = k_ref[0, ks, :]
        return [_chunk_scores(qm_ref, k_tile, r) for r, _ in chunks]

    def update(ks, chunks, logits):
        v = v_ref[0, ks, :]
        suffix2 = suffix_ref[...]
        blocks = [(r, half, on_diagonal) for r, on_diagonal in chunks for half in range(2)]
        z_blocks, sums = [], []
        for r, half, on_diagonal in blocks:
            z = logits[r - chunks[0][0]][half * t:(half + 1) * t]
            neg_log_1m = jnp.maximum(z, 0.0) + jnp.log2(1.0 + jnp.exp2(-jnp.abs(z)))
            if on_diagonal:
                neg_log_1m = jnp.where(_key_before_query(strict=True), neg_log_1m, 0.0)
            z_blocks.append(z)
            sums.append(_dot(jnp.concatenate(_split_bf16(neg_log_1m, 2), axis=1), suffix2))
        for (r, half, on_diagonal), z, s in zip(blocks, z_blocks, sums):
            rows = _half_rows(r, half)
            r_prev = r_ref[rows, :]
            w = jnp.exp2(z + s + jnp.concatenate([r_prev] * (t // LANES), axis=1))
            if on_diagonal:
                w = jnp.where(_key_before_query(strict=True), w, 0.0)
            acc_ref[rows, :] += _dot(w.astype(BF16), v)
            r_ref[rows, :] = r_prev + jnp.broadcast_to(s[:, 0:1], (t, LANES))

    _causal_key_tiles(scores, update, ATTN_TILE, descending=True)
    lane = lax.broadcasted_iota(jnp.int32, (t, LANES), 1)
    for r in range(Q_CHUNKS):
        a0 = acc_ref[(2 * r) * t:(2 * r + 1) * t, :]
        a1 = acc_ref[(2 * r + 1) * t:(2 * r + 2) * t, :]
        o_ref[0, r * t:(r + 1) * t, :] = jnp.where(lane < HEAD_DIM, a0, a1).astype(o_ref.dtype)


def _attention_call(kernel, slab, extra_inputs, extra_specs, n_blocks, q_blk, k_blk, v_blk,
                    scratch, name, query_lanes=LANES):
    b, seq_len, _ = slab.shape
    t = ATTN_TILE
    tq = Q_CHUNKS * t
    q_spec = pl.BlockSpec((1, tq, LANES), lambda bi, h, i: (bi, i, q_blk + h))
    k_spec = pl.BlockSpec((1, seq_len, LANES), lambda bi, h, i: (bi, 0, k_blk + h))
    v_spec = pl.BlockSpec((1, seq_len, LANES), lambda bi, h, i: (bi, 0, v_blk + h))
    return pl.pallas_call(
        kernel,
        grid=(b, n_blocks, seq_len // tq),
        in_specs=[q_spec, k_spec, v_spec] + extra_specs,
        out_specs=pl.BlockSpec((1, tq, LANES), lambda bi, h, i: (bi, i, h)),
        out_shape=jax.ShapeDtypeStruct((b, seq_len, n_blocks * LANES), BF16),
        scratch_shapes=[pltpu.VMEM((2 * tq, query_lanes), BF16)] + scratch,
        compiler_params=pltpu.CompilerParams(
            dimension_semantics=("arbitrary", "arbitrary", "arbitrary"),
            vmem_limit_bytes=VMEM_LIMIT_BYTES),
        name=name,
    )(slab, slab, slab, *extra_inputs)


def _softmax_scratch(acc_lanes):
    rows = 2 * Q_CHUNKS * ATTN_TILE
    return [pltpu.VMEM((rows, LANES), F32), pltpu.VMEM((rows, acc_lanes), F32)]


def _fox_attention(slab, decay):
    decay_spec = pl.BlockSpec((1,) + decay.shape[1:], lambda bi, h, i: (bi, 0, 0))
    return _attention_call(
        _fox_kernel, slab, [decay], [decay_spec], H_FOX // 2,
        OFF_FQ // LANES, OFF_FK // LANES, OFF_FV // LANES, _softmax_scratch(LANES), "fox_attention",
        query_lanes=2 * LANES)


def _diff_attention(slab, lam_params, subln_g, lam_init):
    const = lambda bi, h, i: (0, 0)
    specs = [pl.BlockSpec(lam_params.shape, const), pl.BlockSpec(subln_g.shape, const)]
    return _attention_call(
        functools.partial(_diff_kernel, lam_init), slab, [lam_params, subln_g], specs, H_DIFF,
        OFF_DQ // LANES, OFF_DK // LANES, OFF_DV // LANES, _softmax_scratch(2 * LANES),
        "diff_attention")


def _stick_attention(slab):
    t = ATTN_TILE
    rows = 2 * Q_CHUNKS * t
    scratch = [pltpu.VMEM((2 * t, t), BF16), pltpu.VMEM((rows, LANES), F32),
               pltpu.VMEM((rows, LANES), F32)]
    return _attention_call(
        _stick_kernel, slab, [], [], H_STICK // 2,
        OFF_SQ // LANES, OFF_SK // LANES, OFF_SV // LANES, scratch, "stick_attention")


def _shift_rows(u, prev, n):
    rolled = pltpu.roll(u, n, 0)
    head = rolled[:SUBLANES]
    row = lax.broadcasted_iota(jnp.int32, head.shape, 0)
    for r in range(n):
        src = SUBLANES - n + r
        head = jnp.where(row == r, prev[src:src + 1], head)
    return jnp.concatenate([head, rolled[SUBLANES:]], axis=0)


def _mix_ffn_kernel(tiles_per_seq, final, x_ref, ga_ref, wg_ref, of_ref, od_ref, os_ref, wf_ref,
                    wd_ref, ws_ref, wo_ref, g_ref, wup_ref, cw_ref, cb_ref, wdn_ref, gfin_ref,
                    y_ref, hn_ref, act_ref, tail_ref):
    tm = x_ref.shape[0]
    d = D_MODEL

    @pl.when(pl.program_id(0) % tiles_per_seq == 0)
    def _():
        tail_ref[...] = jnp.zeros(tail_ref.shape, F32)

    x = x_ref[...]
    hn_ref[...] = ((x * _rms_scale(x)) * ga_ref[...]).astype(BF16)
    mixed = None
    for branch, (o_ref, w_ref) in enumerate(((of_ref, wf_ref), (od_ref, wd_ref), (os_ref, ws_ref))):
        gate = jax.nn.sigmoid(_dot(hn_ref[...], wg_ref[:, branch * d:(branch + 1) * d]))
        term = gate * _dot(o_ref[...], w_ref[...])
        mixed = term if mixed is None else mixed + term
    x = x + _dot(mixed.astype(BF16), wo_ref[...])

    hn_ref[...] = ((x * _rms_scale(x)) * g_ref[...]).astype(BF16)
    for c in range(D_FF // FF_CHUNK):
        conv = []
        for part in range(2):
            cols = slice(part * D_FF + c * FF_CHUNK, part * D_FF + (c + 1) * FF_CHUNK)
            u = _dot(hn_ref[...], wup_ref[:, cols])
            prev = tail_ref[:, cols]
            tail_ref[:, cols] = u[tm - SUBLANES:]
            conv.append(cw_ref[0:1, cols] * _shift_rows(u, prev, 2)
                        + cw_ref[1:2, cols] * _shift_rows(u, prev, 1)
                        + cw_ref[2:3, cols] * u + cb_ref[:, cols])
        gate, val = conv
        act_ref[:, c * FF_CHUNK:(c + 1) * FF_CHUNK] = (gate * jax.nn.sigmoid(gate) * val).astype(BF16)
    y = x + _dot(act_ref[...], wdn_ref[...])
    if final:
        y = (y * _rms_scale(y)) * gfin_ref[...]
    y_ref[...] = y


def _mix_ffn(x2d, g_attn, w_gate, o_fox, o_diff, o_stick, w_bf, w_bd, w_bs, w_out,
             g, w_up, conv_w, conv_b, w_down, g_final, seq_len, final):
    n_tok = x2d.shape[0]
    tm = TOKEN_TILE
    const = lambda i: (0, 0)
    rows = lambda i: (i, 0)
    resident = lambda w: pl.BlockSpec(w.shape, const, pipeline_mode=pl.Buffered(1))
    return pl.pallas_call(
        functools.partial(_mix_ffn_kernel, seq_len // tm, final),
        grid=(n_tok // tm,),
        in_specs=[
            pl.BlockSpec((tm, D_MODEL), rows),
            pl.BlockSpec((1, D_MODEL), const),
            resident(w_gate),
            pl.BlockSpec((tm, W_FOX), rows),
            pl.BlockSpec((tm, W_DIFF), rows),
            pl.BlockSpec((tm, W_STICK), rows),
            resident(w_bf), resident(w_bd), resident(w_bs), resident(w_out),
            pl.BlockSpec((1, D_MODEL), const),
            resident(w_up),
            pl.BlockSpec(conv_w.shape, const),
            pl.BlockSpec(conv_b.shape, const),
            resident(w_down),
            pl.BlockSpec((1, D_MODEL), const),
        ],
        out_specs=pl.BlockSpec((tm, D_MODEL), rows),
        out_shape=jax.ShapeDtypeStruct((n_tok, D_MODEL), F32),
        scratch_shapes=[
            pltpu.VMEM((tm, D_MODEL), BF16),
            pltpu.VMEM((tm, D_FF), BF16),
            pltpu.VMEM((SUBLANES, 2 * D_FF), F32),
        ],
        compiler_params=pltpu.CompilerParams(
            dimension_semantics=("arbitrary",), vmem_limit_bytes=VMEM_LIMIT_BYTES),
        name="mix_ffn",
    )(x2d, g_attn, w_gate, o_fox, o_diff, o_stick, w_bf, w_bd, w_bs, w_out,
      g, w_up, conv_w, conv_b, w_down, g_final)


def _rope_tables(seq_len):
    half = ROPE_DIM // 2
    pos = jnp.arange(seq_len, dtype=F32)
    inv_freq = ROPE_THETA ** (-jnp.arange(0, ROPE_DIM, 2, dtype=F32) / ROPE_DIM)
    ang = pos[:, None] * inv_freq[None, :]
    cos, sin = jnp.cos(ang), jnp.sin(ang)
    ones = jnp.ones((seq_len, HEAD_DIM - ROPE_DIM), F32)
    zeros = jnp.zeros((seq_len, HEAD_DIM - ROPE_DIM), F32)
    zh = jnp.zeros((seq_len, half), F32)
    cos_h = jnp.concatenate([cos, cos, ones], axis=-1)
    sup_h = jnp.concatenate([zh, sin, zeros], axis=-1)
    sdn_h = jnp.concatenate([-sin, zh, zeros], axis=-1)
    rep = LANES // HEAD_DIM
    return tuple(jnp.tile(t, (1, rep)) for t in (cos_h, sup_h, sdn_h))


def _slab_weight(w_in_l):
    scale = jnp.ones((N_SLAB,), F32)
    for off, width in ((OFF_FQ, W_FOX), (OFF_DQ, W_DIFF_QK), (OFF_SQ, W_STICK)):
        scale = scale.at[off:off + width].set(QK_SCALE * LOG2_E)
    return (w_in_l[:, :N_SLAB] * scale[None, :]).astype(BF16)


def _forget_weight(w_in_l, forget_bias_l):
    w_f = jnp.tile(w_in_l[:, IN_OFF_FORGET:IN_OFF_FORGET + H_FOX], (1, 3))
    w_f = jnp.pad(w_f, ((0, 0), (0, LANES - 3 * H_FOX))).astype(BF16)
    bias = jnp.pad(jnp.tile(forget_bias_l.astype(F32), 3), (0, LANES - 3 * H_FOX)).reshape(1, LANES)
    return w_f, bias


def kernel(x, attn_norm_g, w_in, forget_bias, lam_q1, lam_k1, lam_q2, lam_k2, diff_subln_g,
           w_br_fox, w_br_diff, w_br_stick, w_out, ffn_norm_g, w_up, conv_w, conv_b, w_down,
           final_norm_g):
    b, seq_len, d = x.shape
    depth = w_in.shape[0]
    n_tok = b * seq_len
    cos_t, sup_t, sdn_t = _rope_tables(seq_len)
    x2d = x.reshape(n_tok, d)
    g_final = final_norm_g.reshape(1, d)

    for l in range(depth):
        lam_init = 0.8 - 0.6 * math.exp(-0.3 * l)
        w_f, f_bias = _forget_weight(w_in[l], forget_bias[l])
        slab2d, decay = _inproj(x2d, attn_norm_g[l].reshape(1, d), _slab_weight(w_in[l]), w_f, f_bias,
                                cos_t, sup_t, sdn_t, seq_len)
        slab = slab2d.reshape(b, seq_len, N_SLAB)

        o_fox = _fox_attention(slab, decay.reshape(b, seq_len, LANES))
        lam_params = jnp.stack([lam_q1[l], lam_k1[l], lam_q2[l], lam_k2[l]]).astype(F32)
        o_diff = _diff_attention(slab, lam_params, diff_subln_g[l].reshape(1, -1), lam_init)
        o_stick = _stick_attention(slab)

        x2d = _mix_ffn(x2d, attn_norm_g[l].reshape(1, d), w_in[l][:, IN_OFF_GATE:].astype(BF16),
                       o_fox.reshape(n_tok, -1), o_diff.reshape(n_tok, -1), o_stick.reshape(n_tok, -1),
                       w_br_fox[l].astype(BF16), w_br_diff[l].astype(BF16), w_br_stick[l].astype(BF16),
                       w_out[l].astype(BF16), ffn_norm_g[l].reshape(1, d), w_up[l].astype(BF16),
                       conv_w[l], conv_b[l].reshape(1, -1), w_down[l].astype(BF16), g_final,
                       seq_len, final=(l == depth - 1))
    return x2d.reshape(b, seq_len, d)
```

```python
import functools
import math

import jax
import jax.numpy as jnp
from jax import lax
from jax.experimental import pallas as pl
from jax.experimental.pallas import tpu as pltpu

F32 = jnp.float32
BF16 = jnp.bfloat16

D_MODEL = 1024
HEAD_DIM = 64
H_FOX = 4
H_DIFF = 4
H_STICK = 4
W_FOX = H_FOX * HEAD_DIM
W_DIFF_QK = 2 * H_DIFF * HEAD_DIM
W_DIFF = H_DIFF * 2 * HEAD_DIM
W_STICK = H_STICK * HEAD_DIM
N_BRANCH = 3
ROPE_THETA = 500000.0
ROPE_DIM = HEAD_DIM // 4
D_FF = 2816
CONV_WIDTH = 3
NORM_EPS = 1e-6
QK_SCALE = HEAD_DIM ** -0.5
LOG2_E = math.log2(math.e)

LANES = 128
SUBLANES = 8
VMEM_LIMIT_BYTES = 56 * 1024 * 1024

OFF_FQ = 0
OFF_FK = OFF_FQ + W_FOX
OFF_FV = OFF_FK + W_FOX
OFF_DQ = OFF_FV + W_FOX
OFF_DK = OFF_DQ + W_DIFF_QK
OFF_DV = OFF_DK + W_DIFF_QK
OFF_SQ = OFF_DV + W_DIFF
OFF_SK = OFF_SQ + W_STICK
OFF_SV = OFF_SK + W_STICK
N_SLAB = OFF_SV + W_STICK
IN_OFF_FORGET = N_SLAB
IN_OFF_GATE = IN_OFF_FORGET + H_FOX

TOKEN_TILE = 512
ATTN_TILE = 256
Q_CHUNKS = 4
SOFTMAX_OFF_KEYS = 1024
STICK_OFF_KEYS = 512
FF_CHUNK = 256


def _log_sigmoid_parts(z):
    l1p = jnp.log(1.0 + jnp.exp(-jnp.abs(z)))
    return jnp.minimum(z, 0.0) - l1p, -jnp.maximum(z, 0.0) - l1p


def _rms_scale(x):
    return lax.rsqrt(jnp.mean(x * x, axis=-1, keepdims=True) + NORM_EPS)


def _split_bf16(v, n_parts):
    parts = []
    rem = v
    for _ in range(n_parts):
        p = rem.astype(BF16)
        parts.append(p)
        rem = rem - p.astype(F32)
    return parts


def _dot_nt(a, b):
    return lax.dot_general(a, b, (((1,), (1,)), ((), ())), preferred_element_type=F32)


def _dot(a, b):
    return jnp.dot(a, b, preferred_element_type=F32)


_PROJ_CHUNKS = (
    [(OFF_FQ, 3 * W_FOX // 2, "plain"), (OFF_FQ + 3 * W_FOX // 2, 3 * W_FOX // 2, "plain")]
    + [(OFF_DQ, W_DIFF_QK, "rope"), (OFF_DK, W_DIFF_QK, "rope"), (OFF_DV, W_DIFF, "plain")]
    + [(OFF_SQ, 3 * W_STICK // 2, "plain"), (OFF_SQ + 3 * W_STICK // 2, 3 * W_STICK // 2, "plain")]
)


def _rope_lanes(t, cos, sin_up, sin_dn):
    return (t * cos + pltpu.roll(t, ROPE_DIM // 2, 1) * sin_up
            + pltpu.roll(t, LANES - ROPE_DIM // 2, 1) * sin_dn)


def _inproj_kernel(tiles_per_seq, x_ref, g_ref, w_ref, wf_ref, fb_ref, tri_ref, cos_ref, sup_ref,
                   sdn_ref, slab_ref, decay_ref, xn_ref, carry_ref):
    @pl.when(pl.program_id(0) % tiles_per_seq == 0)
    def _():
        carry_ref[...] = jnp.zeros(carry_ref.shape, F32)

    x = x_ref[...]
    xn_ref[...] = ((x * _rms_scale(x)) * g_ref[...]).astype(BF16)
    for off, width, kind in _PROJ_CHUNKS:
        res = _dot(xn_ref[...], w_ref[:, off:off + width])
        if kind == "rope":
            cos, sup, sdn = cos_ref[...], sup_ref[...], sdn_ref[...]
            for c in range(width // LANES):
                blk = _rope_lanes(res[:, c * LANES:(c + 1) * LANES], cos, sup, sdn)
                slab_ref[:, off + c * LANES:off + (c + 1) * LANES] = blk.astype(BF16)
        else:
            slab_ref[:, off:off + width] = res.astype(BF16)
    log_f = _log_sigmoid_parts(_dot(xn_ref[...], wf_ref[...]) + fb_ref[...])[0]
    tri = tri_ref[...]
    c = carry_ref[0:1, :]
    for part in _split_bf16(log_f, 3):
        c = c + _dot(tri, part)
    tm = c.shape[0]
    carry_ref[...] = jnp.broadcast_to(c[tm - 1:tm, :], carry_ref.shape)
    hi, mid, lo = _split_bf16(c * (-LOG2_E), 3)
    lane = lax.broadcasted_iota(jnp.int32, c.shape, 1)
    part = jnp.where(lane < H_FOX, hi, jnp.where(lane < 2 * H_FOX, mid, lo))
    decay_ref[...] = jnp.where(lane < 3 * H_FOX, part, jnp.zeros_like(part))


def _inproj(x2d, g, w_slab, w_f, f_bias, cos_t, sup_t, sdn_t, seq_len):
    n_tok = x2d.shape[0]
    tm = TOKEN_TILE
    tiles_per_seq = seq_len // tm
    const = lambda i: (0, 0)
    rows = lambda i: (i, 0)
    rope_map = lambda i: (i % tiles_per_seq, 0)
    tri = jnp.tril(jnp.ones((tm, tm), BF16))
    return pl.pallas_call(
        functools.partial(_inproj_kernel, tiles_per_seq),
        grid=(n_tok // tm,),
        in_specs=[
            pl.BlockSpec((tm, D_MODEL), rows),
            pl.BlockSpec((1, D_MODEL), const),
            pl.BlockSpec((D_MODEL, N_SLAB), const, pipeline_mode=pl.Buffered(1)),
            pl.BlockSpec((D_MODEL, LANES), const, pipeline_mode=pl.Buffered(1)),
            pl.BlockSpec((1, LANES), const),
            pl.BlockSpec((tm, tm), const, pipeline_mode=pl.Buffered(1)),
            pl.BlockSpec((tm, LANES), rope_map),
            pl.BlockSpec((tm, LANES), rope_map),
            pl.BlockSpec((tm, LANES), rope_map),
        ],
        out_specs=[
            pl.BlockSpec((tm, N_SLAB), rows),
            pl.BlockSpec((tm, LANES), rows),
        ],
        out_shape=[
            jax.ShapeDtypeStruct((n_tok, N_SLAB), BF16),
            jax.ShapeDtypeStruct((n_tok, LANES), BF16),
        ],
        scratch_shapes=[pltpu.VMEM((tm, D_MODEL), BF16), pltpu.VMEM((SUBLANES, LANES), F32)],
        compiler_params=pltpu.CompilerParams(
            dimension_semantics=("arbitrary",), vmem_limit_bytes=VMEM_LIMIT_BYTES),
        name="inproj",
    )(x2d, g, w_slab, w_f, f_bias, tri, cos_t, sup_t, sdn_t)


def _stack_query_halves(q_ref, qm_ref):
    t = ATTN_TILE
    lane = lax.broadcasted_iota(jnp.int32, (t, LANES), 1)
    for r in range(Q_CHUNKS):
        q = q_ref[0, r * t:(r + 1) * t, :]
        zero = jnp.zeros_like(q)
        qm_ref[(2 * r) * t:(2 * r + 1) * t, 0:LANES] = jnp.where(lane < HEAD_DIM, q, zero)
        qm_ref[(2 * r + 1) * t:(2 * r + 2) * t, 0:LANES] = jnp.where(lane >= HEAD_DIM, q, zero)


def _key_before_query(strict):
    query = lax.broadcasted_iota(jnp.int32, (ATTN_TILE, ATTN_TILE), 0)
    key = lax.broadcasted_iota(jnp.int32, (ATTN_TILE, ATTN_TILE), 1)
    return key < query if strict else key <= query


def _causal_key_tiles(scores_fn, update_fn, off_keys, descending):
    t = ATTN_TILE
    block_start = Q_CHUNKS * t * pl.program_id(2)
    n_off = pl.program_id(2) * (Q_CHUNKS * t // off_keys)

    def step(ks, chunks):
        update_fn(ks, chunks, scores_fn(ks, chunks))

    def off_diagonal(jj, carry):
        j = n_off - 1 - jj if descending else jj
        step(pl.ds(pl.multiple_of(j * off_keys, off_keys), off_keys), _visible_chunks(None))
        return carry

    def diagonal(d):
        step(pl.ds(pl.multiple_of(block_start + d * t, t), t), _visible_chunks(d))

    if descending:
        for d in reversed(range(Q_CHUNKS)):
            diagonal(d)
        lax.fori_loop(0, n_off, off_diagonal, 0)
    else:
        lax.fori_loop(0, n_off, off_diagonal, 0)
        for d in range(Q_CHUNKS):
            diagonal(d)


def _visible_chunks(diag_chunk):
    first = 0 if diag_chunk is None else diag_chunk
    return [(r, r == diag_chunk) for r in range(first, Q_CHUNKS)]


def _half_rows(r, half):
    return slice((2 * r + half) * ATTN_TILE, (2 * r + half + 1) * ATTN_TILE)


def _chunk_scores(qm_ref, k_tile, chunks):
    t2 = 2 * ATTN_TILE
    return [_dot_nt(qm_ref[r * t2:(r + 1) * t2, :], k_tile) for r, _ in chunks]


def _softmax_update(s, on_diagonal, v_aug, rows, m_ref, acc_ref):
    if on_diagonal:
        s = jnp.where(_key_before_query(strict=False), s, -jnp.inf)
    m_prev = m_ref[rows, :]
    m_new = jnp.maximum(m_prev, jnp.max(s, axis=-1, keepdims=True))
    p = jnp.exp2(s - jnp.concatenate([m_new] * (s.shape[1] // LANES), axis=1)).astype(BF16)
    alpha = jnp.exp2(m_prev - m_new)
    if acc_ref.shape[1] != LANES:
        alpha = jnp.concatenate([alpha] * (acc_ref.shape[1] // LANES), axis=1)
    acc_ref[rows, :] = alpha * acc_ref[rows, :] + _dot(p, v_aug)
    m_ref[rows, :] = m_new


def _init_softmax_state(m_ref, acc_ref):
    m_ref[...] = jnp.full(m_ref.shape, -jnp.inf, F32)
    acc_ref[...] = jnp.zeros(acc_ref.shape, F32)


def _fox_kernel(q_ref, k_ref, v_ref, decay_ref, o_ref, qm_ref, m_ref, acc_ref):
    t = ATTN_TILE
    _stack_query_halves(q_ref, qm_ref)
    _init_softmax_state(m_ref, acc_ref)
    lane = lax.broadcasted_iota(jnp.int32, (t, LANES), 1)
    for half in range(2):
        head = 2 * pl.program_id(1) + half
        own = (lane == head) | (lane == head + H_FOX) | (lane == head + 2 * H_FOX)
        ones_in_own_lanes = jnp.where(own, 1.0, 0.0).astype(BF16)
        for r in range(Q_CHUNKS):
            qm_ref[_half_rows(r, half), LANES:2 * LANES] = ones_in_own_lanes

    def scores(ks, chunks):
        k_aug = jnp.concatenate([k_ref[0, ks, :], decay_ref[0, ks, :]], axis=1)
        return _chunk_scores(qm_ref, k_aug, chunks)

    def update(ks, chunks, score_list):
        v = v_ref[0, ks, :]
        one = jnp.ones_like(v)
        lane_k = lax.broadcasted_iota(jnp.int32, v.shape, 1)
        v_aug = [jnp.where(lane_k < HEAD_DIM, v, one), jnp.where(lane_k >= HEAD_DIM, v, one)]
        for (r, on_diagonal), s in zip(chunks, score_list):
            for half in range(2):
                _softmax_update(s[half * t:(half + 1) * t], on_diagonal, v_aug[half],
                                _half_rows(r, half), m_ref, acc_ref)

    _causal_key_tiles(scores, update, SOFTMAX_OFF_KEYS, descending=False)
    for r in range(Q_CHUNKS):
        a0 = acc_ref[(2 * r) * t:(2 * r + 1) * t, :]
        a1 = acc_ref[(2 * r + 1) * t:(2 * r + 2) * t, :]
        num = jnp.where(lane < HEAD_DIM, a0, a1)
        den = jnp.where(lane < HEAD_DIM, pltpu.roll(a0, HEAD_DIM, 1), pltpu.roll(a1, HEAD_DIM, 1))
        o_ref[0, r * t:(r + 1) * t, :] = (num / den).astype(o_ref.dtype)


def _diff_kernel(lam_init, q_ref, k_ref, v_ref, lam_ref, g_ref, o_ref, qm_ref, m_ref, acc_ref):
    t = ATTN_TILE
    dv = 2 * HEAD_DIM
    _stack_query_halves(q_ref, qm_ref)
    _init_softmax_state(m_ref, acc_ref)

    def scores(ks, chunks):
        return _chunk_scores(qm_ref, k_ref[0, ks, :], chunks)

    def update(ks, chunks, score_list):
        v = v_ref[0, ks, :]
        v_aug = jnp.concatenate([v, jnp.ones_like(v)], axis=1)
        for (r, on_diagonal), s in zip(chunks, score_list):
            for half in range(2):
                _softmax_update(s[half * t:(half + 1) * t], on_diagonal, v_aug,
                                _half_rows(r, half), m_ref, acc_ref)

    _causal_key_tiles(scores, update, SOFTMAX_OFF_KEYS, descending=False)
    lp = lam_ref[...]
    lam = (jnp.exp(jnp.sum(lp[0:1] * lp[1:2], axis=-1, keepdims=True))
           - jnp.exp(jnp.sum(lp[2:3] * lp[3:4], axis=-1, keepdims=True)) + lam_init)
    for r in range(Q_CHUNKS):
        a1 = acc_ref[(2 * r) * t:(2 * r + 1) * t, :]
        a2 = acc_ref[(2 * r + 1) * t:(2 * r + 2) * t, :]
        o = a1[:, :dv] / a1[:, dv:] - lam * (a2[:, :dv] / a2[:, dv:])
        o = (o * _rms_scale(o)) * g_ref[...]
        o_ref[0, r * t:(r + 1) * t, :] = (o * (1.0 - lam_init)).astype(o_ref.dtype)


def _stick_kernel(q_ref, k_ref, v_ref, o_ref, qm_ref, suffix_ref, r_ref, acc_ref):
    t = ATTN_TILE
    _stack_query_halves(q_ref, qm_ref)
    r_ref[...] = jnp.zeros(r_ref.shape, F32)
    acc_ref[...] = jnp.zeros(acc_ref.shape, F32)
    row = lax.broadcasted_iota(jnp.int32, (2 * t, t), 0)
    col = lax.broadcasted_iota(jnp.int32, (2 * t, t), 1)
    suffix_ref[...] = jnp.where((row >= col) & ((row < t) | (row >= col + t)), -1.0, 0.0).astype(BF16)

    def scores(ks, chunks):
        return _chunk_scores(qm_ref, k_ref[0, ks, :], chunks)

    def update(ks, chunks, logits):
        v = v_ref[0, ks, :]
        suffix2 = suffix_ref[...]
        n_sub = v.shape[0] // t
        blocks = [(r, half, on_diagonal) for r, on_diagonal in chunks for half in range(2)]
        z_blocks, sums = [], []
        for r, half, on_diagonal in blocks:
            z = logits[r - chunks[0][0]][half * t:(half + 1) * t]
            neg_log_1m = jnp.maximum(z, 0.0) + jnp.log2(1.0 + jnp.exp2(-jnp.abs(z)))
            if on_diagonal:
                neg_log_1m = jnp.where(_key_before_query(strict=True), neg_log_1m, 0.0)
            z_blocks.append(z)
            sums.append([
                _dot(jnp.concatenate(_split_bf16(neg_log_1m[:, u * t:(u + 1) * t], 2), axis=1), suffix2)
                for u in range(n_sub)])
        for (r, half, on_diagonal), z, s in zip(blocks, z_blocks, sums):
            rows = _half_rows(r, half)
            r_run = r_ref[rows, :]
            w = [None] * n_sub
            for u in reversed(range(n_sub)):
                w_u = jnp.exp2(z[:, u * t:(u + 1) * t] + s[u]
                               + jnp.concatenate([r_run] * (t // LANES), axis=1))
                if on_diagonal:
                    w_u = jnp.where(_key_before_query(strict=True), w_u, 0.0)
                w[u] = w_u.astype(BF16)
                r_run = r_run + jnp.broadcast_to(s[u][:, 0:1], (t, LANES))
            acc_ref[rows, :] += _dot(jnp.concatenate(w, axis=1), v)
            r_ref[rows, :] = r_run

    _causal_key_tiles(scores, update, STICK_OFF_KEYS, descending=True)
    lane = lax.broadcasted_iota(jnp.int32, (t, LANES), 1)
    for r in range(Q_CHUNKS):
        a0 = acc_ref[(2 * r) * t:(2 * r + 1) * t, :]
        a1 = acc_ref[(2 * r + 1) * t:(2 * r + 2) * t, :]
        o_ref[0, r * t:(r + 1) * t, :] = jnp.where(lane < HEAD_DIM, a0, a1).astype(o_ref.dtype)


def _attention_call(kernel, slab, extra_inputs, extra_specs, n_blocks, q_blk, k_blk, v_blk,
                    scratch, name, query_lanes=LANES):
    b, seq_len, _ = slab.shape
    t = ATTN_TILE
    tq = Q_CHUNKS * t
    q_spec = pl.BlockSpec((1, tq, LANES), lambda bi, h, i: (bi, i, q_blk + h))
    k_spec = pl.BlockSpec((1, seq_len, LANES), lambda bi, h, i: (bi, 0, k_blk + h))
    v_spec = pl.BlockSpec((1, seq_len, LANES), lambda bi, h, i: (bi, 0, v_blk + h))
    return pl.pallas_call(
        kernel,
        grid=(b, n_blocks, seq_len // tq),
        in_specs=[q_spec, k_spec, v_spec] + extra_specs,
        out_specs=pl.BlockSpec((1, tq, LANES), lambda bi, h, i: (bi, i, h)),
        out_shape=jax.ShapeDtypeStruct((b, seq_len, n_blocks * LANES), BF16),
        scratch_shapes=[pltpu.VMEM((2 * tq, query_lanes), BF16)] + scratch,
        compiler_params=pltpu.CompilerParams(
            dimension_semantics=("arbitrary", "arbitrary", "arbitrary"),
            vmem_limit_bytes=VMEM_LIMIT_BYTES),
        name=name,
    )(slab, slab, slab, *extra_inputs)


def _softmax_scratch(acc_lanes):
    rows = 2 * Q_CHUNKS * ATTN_TILE
    return [pltpu.VMEM((rows, LANES), F32), pltpu.VMEM((rows, acc_lanes), F32)]


def _fox_attention(slab, decay):
    decay_spec = pl.BlockSpec((1,) + decay.shape[1:], lambda bi, h, i: (bi, 0, 0))
    return _attention_call(
        _fox_kernel, slab, [decay], [decay_spec], H_FOX // 2,
        OFF_FQ // LANES, OFF_FK // LANES, OFF_FV // LANES, _softmax_scratch(LANES), "fox_attention",
        query_lanes=2 * LANES)


def _diff_attention(slab, lam_params, subln_g, lam_init):
    const = lambda bi, h, i: (0, 0)
    specs = [pl.BlockSpec(lam_params.shape, const), pl.BlockSpec(subln_g.shape, const)]
    return _attention_call(
        functools.partial(_diff_kernel, lam_init), slab, [lam_params, subln_g], specs, H_DIFF,
        OFF_DQ // LANES, OFF_DK // LANES, OFF_DV // LANES, _softmax_scratch(2 * LANES),
        "diff_attention")


def _stick_attention(slab):
    t = ATTN_TILE
    rows = 2 * Q_CHUNKS * t
    scratch = [pltpu.VMEM((2 * t, t), BF16), pltpu.VMEM((rows, LANES), F32),
               pltpu.VMEM((rows, LANES), F32)]
    return _attention_call(
        _stick_kernel, slab, [], [], H_STICK // 2,
        OFF_SQ // LANES, OFF_SK // LANES, OFF_SV // LANES, scratch, "stick_attention")


def _shift_rows(u, prev, n):
    rolled = pltpu.roll(u, n, 0)
    head = rolled[:SUBLANES]
    row = lax.broadcasted_iota(jnp.int32, head.shape, 0)
    for r in range(n):
        src = SUBLANES - n + r
        head = jnp.where(row == r, prev[src:src + 1], head)
    return jnp.concatenate([head, rolled[SUBLANES:]], axis=0)


def _mix_ffn_kernel(tiles_per_seq, final, x_ref, ga_ref, wg_ref, of_ref, od_ref, os_ref, wf_ref,
                    wd_ref, ws_ref, wo_ref, g_ref, wup_ref, cw_ref, cb_ref, wdn_ref, gfin_ref,
                    y_ref, hn_ref, act_ref, tail_ref):
    tm = x_ref.shape[0]
    d = D_MODEL

    @pl.when(pl.program_id(0) % tiles_per_seq == 0)
    def _():
        tail_ref[...] = jnp.zeros(tail_ref.shape, F32)

    x = x_ref[...]
    hn_ref[...] = ((x * _rms_scale(x)) * ga_ref[...]).astype(BF16)
    mixed = None
    for branch, (o_ref, w_ref) in enumerate(((of_ref, wf_ref), (od_ref, wd_ref), (os_ref, ws_ref))):
        gate = jax.nn.sigmoid(_dot(hn_ref[...], wg_ref[:, branch * d:(branch + 1) * d]))
        term = gate * _dot(o_ref[...], w_ref[...])
        mixed = term if mixed is None else mixed + term
    x = x + _dot(mixed.astype(BF16), wo_ref[...])

    hn_ref[...] = ((x * _rms_scale(x)) * g_ref[...]).astype(BF16)
    for c in range(D_FF // FF_CHUNK):
        conv = []
        for part in range(2):
            cols = slice(part * D_FF + c * FF_CHUNK, part * D_FF + (c + 1) * FF_CHUNK)
            u = _dot(hn_ref[...], wup_ref[:, cols])
            prev = tail_ref[:, cols]
            tail_ref[:, cols] = u[tm - SUBLANES:]
            conv.append(cw_ref[0:1, cols] * _shift_rows(u, prev, 2)
                        + cw_ref[1:2, cols] * _shift_rows(u, prev, 1)
                        + cw_ref[2:3, cols] * u + cb_ref[:, cols])
        gate, val = conv
        act_ref[:, c * FF_CHUNK:(c + 1) * FF_CHUNK] = (gate * jax.nn.sigmoid(gate) * val).astype(BF16)
    y = x + _dot(act_ref[...], wdn_ref[...])
    if final:
        y = (y * _rms_scale(y)) * gfin_ref[...]
    y_ref[...] = y


def _mix_ffn(x2d, g_attn, w_gate, o_fox, o_diff, o_stick, w_bf, w_bd, w_bs, w_out,
             g, w_up, conv_w, conv_b, w_down, g_final, seq_len, final):
    n_tok = x2d.shape[0]
    tm = TOKEN_TILE
    const = lambda i: (0, 0)
    rows = lambda i: (i, 0)
    resident = lambda w: pl.BlockSpec(w.shape, const, pipeline_mode=pl.Buffered(1))
    return pl.pallas_call(
        functools.partial(_mix_ffn_kernel, seq_len // tm, final),
        grid=(n_tok // tm,),
        in_specs=[
            pl.BlockSpec((tm, D_MODEL), rows),
            pl.BlockSpec((1, D_MODEL), const),
            resident(w_gate),
            pl.BlockSpec((tm, W_FOX), rows),
            pl.BlockSpec((tm, W_DIFF), rows),
            pl.BlockSpec((tm, W_STICK), rows),
            resident(w_bf), resident(w_bd), resident(w_bs), resident(w_out),
            pl.BlockSpec((1, D_MODEL), const),
            resident(w_up),
            pl.BlockSpec(conv_w.shape, const),
            pl.BlockSpec(conv_b.shape, const),
            resident(w_down),
            pl.BlockSpec((1, D_MODEL), const),
        ],
        out_specs=pl.BlockSpec((tm, D_MODEL), rows),
        out_shape=jax.ShapeDtypeStruct((n_tok, D_MODEL), F32),
        scratch_shapes=[
            pltpu.VMEM((tm, D_MODEL), BF16),
            pltpu.VMEM((tm, D_FF), BF16),
            pltpu.VMEM((SUBLANES, 2 * D_FF), F32),
        ],
        compiler_params=pltpu.CompilerParams(
            dimension_semantics=("arbitrary",), vmem_limit_bytes=VMEM_LIMIT_BYTES),
        name="mix_ffn",
    )(x2d, g_attn, w_gate, o_fox, o_diff, o_stick, w_bf, w_bd, w_bs, w_out,
      g, w_up, conv_w, conv_b, w_down, g_final)


def _rope_tables(seq_len):
    half = ROPE_DIM // 2
    pos = jnp.arange(seq_len, dtype=F32)
    inv_freq = ROPE_THETA ** (-jnp.arange(0, ROPE_DIM, 2, dtype=F32) / ROPE_DIM)
    ang = pos[:, None] * inv_freq[None, :]
    cos, sin = jnp.cos(ang), jnp.sin(ang)
    ones = jnp.ones((seq_len, HEAD_DIM - ROPE_DIM), F32)
    zeros = jnp.zeros((seq_len, HEAD_DIM - ROPE_DIM), F32)
    zh = jnp.zeros((seq_len, half), F32)
    cos_h = jnp.concatenate([cos, cos, ones], axis=-1)
    sup_h = jnp.concatenate([zh, sin, zeros], axis=-1)
    sdn_h = jnp.concatenate([-sin, zh, zeros], axis=-1)
    rep = LANES // HEAD_DIM
    return tuple(jnp.tile(t, (1, rep)) for t in (cos_h, sup_h, sdn_h))


def _slab_weight(w_in_l):
    scale = jnp.ones((N_SLAB,), F32)
    for off, width in ((OFF_FQ, W_FOX), (OFF_DQ, W_DIFF_QK), (OFF_SQ, W_STICK)):
        scale = scale.at[off:off + width].set(QK_SCALE * LOG2_E)
    return (w_in_l[:, :N_SLAB] * scale[None, :]).astype(BF16)


def _forget_weight(w_in_l, forget_bias_l):
    w_f = jnp.tile(w_in_l[:, IN_OFF_FORGET:IN_OFF_FORGET + H_FOX], (1, 3))
    w_f = jnp.pad(w_f, ((0, 0), (0, LANES - 3 * H_FOX))).astype(BF16)
    bias = jnp.pad(jnp.tile(forget_bias_l.astype(F32), 3), (0, LANES - 3 * H_FOX)).reshape(1, LANES)
    return w_f, bias


def kernel(x, attn_norm_g, w_in, forget_bias, lam_q1, lam_k1, lam_q2, lam_k2, diff_subln_g,
           w_br_fox, w_br_diff, w_br_stick, w_out, ffn_norm_g, w_up, conv_w, conv_b, w_down,
           final_norm_g):
    b, seq_len, d = x.shape
    depth = w_in.shape[0]
    n_tok = b * seq_len
    cos_t, sup_t, sdn_t = _rope_tables(seq_len)
    x2d = x.reshape(n_tok, d)
    g_final = final_norm_g.reshape(1, d)

    for l in range(depth):
        lam_init = 0.8 - 0.6 * math.exp(-0.3 * l)
        w_f, f_bias = _forget_weight(w_in[l], forget_bias[l])
        slab2d, decay = _inproj(x2d, attn_norm_g[l].reshape(1, d), _slab_weight(w_in[l]), w_f, f_bias,
                                cos_t, sup_t, sdn_t, seq_len)
        slab = slab2d.reshape(b, seq_len, N_SLAB)

        o_fox = _fox_attention(slab, decay.reshape(b, seq_len, LANES))
        lam_params = jnp.stack([lam_q1[l], lam_k1[l], lam_q2[l], lam_k2[l]]).astype(F32)
        o_diff = _diff_attention(slab, lam_params, diff_subln_g[l].reshape(1, -1), lam_init)
        o_stick = _stick_attention(slab)

        x2d = _mix_ffn(x2d, attn_norm_g[l].reshape(1, d), w_in[l][:, IN_OFF_GATE:].astype(BF16),
                       o_fox.reshape(n_tok, -1), o_diff.reshape(n_tok, -1), o_stick.reshape(n_tok, -1),
                       w_br_fox[l].astype(BF16), w_br_diff[l].astype(BF16), w_br_stick[l].astype(BF16),
                       w_out[l].astype(BF16), ffn_norm_g[l].reshape(1, d), w_up[l].astype(BF16),
                       conv_w[l], conv_b[l].reshape(1, -1), w_down[l].astype(BF16), g_final,
                       seq_len, final=(l == depth - 1))
    return x2d.reshape(b, seq_len, d)
```

```python
import functools
import math

import jax
import jax.numpy as jnp
from jax import lax
from jax.experimental import pallas as pl
from jax.experimental.pallas import tpu as pltpu

F32 = jnp.float32
BF16 = jnp.bfloat16

D_MODEL = 1024
HEAD_DIM = 64
H_FOX = 4
H_DIFF = 4
H_STICK = 4
W_FOX = H_FOX * HEAD_DIM
W_DIFF_QK = 2 * H_DIFF * HEAD_DIM
W_DIFF = H_DIFF * 2 * HEAD_DIM
W_STICK = H_STICK * HEAD_DIM
N_BRANCH = 3
ROPE_THETA = 500000.0
ROPE_DIM = HEAD_DIM // 4
D_FF = 2816
CONV_WIDTH = 3
NORM_EPS = 1e-6
QK_SCALE = HEAD_DIM ** -0.5
LOG2_E = math.log2(math.e)

LANES = 128
SUBLANES = 8
VMEM_LIMIT_BYTES = 56 * 1024 * 1024

OFF_FQ = 0
OFF_FK = OFF_FQ + W_FOX
OFF_FV = OFF_FK + W_FOX
OFF_DQ = OFF_FV + W_FOX
OFF_DK = OFF_DQ + W_DIFF_QK
OFF_DV = OFF_DK + W_DIFF_QK
OFF_SQ = OFF_DV + W_DIFF
OFF_SK = OFF_SQ + W_STICK
OFF_SV = OFF_SK + W_STICK
N_SLAB = OFF_SV + W_STICK
IN_OFF_FORGET = N_SLAB
IN_OFF_GATE = IN_OFF_FORGET + H_FOX

TOKEN_TILE = 512
ATTN_TILE = 256
SUFFIX_TILE = 256
Q_CHUNKS = 4
ATTN_GROUP = 1
_STATE_ROWS = ATTN_GROUP * Q_CHUNKS * 2 * ATTN_TILE
SOFTMAX_OFF_KEYS = 1024
STICK_OFF_KEYS = 512
FF_CHUNK = 256


def _log_sigmoid_parts(z):
    l1p = jnp.log(1.0 + jnp.exp(-jnp.abs(z)))
    return jnp.minimum(z, 0.0) - l1p, -jnp.maximum(z, 0.0) - l1p


def _rms_scale(x):
    return lax.rsqrt(jnp.mean(x * x, axis=-1, keepdims=True) + NORM_EPS)


def _split_bf16(v, n_parts):
    parts = []
    rem = v
    for _ in range(n_parts):
        p = rem.astype(BF16)
        parts.append(p)
        rem = rem - p.astype(F32)
    return parts


def _dot_nt(a, b):
    return lax.dot_general(a, b, (((1,), (1,)), ((), ())), preferred_element_type=F32)


def _dot(a, b):
    return jnp.dot(a, b, preferred_element_type=F32)


_PROJ_CHUNKS = (
    [(OFF_FQ, 3 * W_FOX // 2, "plain"), (OFF_FQ + 3 * W_FOX // 2, 3 * W_FOX // 2, "plain")]
    + [(OFF_DQ, W_DIFF_QK, "rope"), (OFF_DK, W_DIFF_QK, "rope"), (OFF_DV, W_DIFF, "plain")]
    + [(OFF_SQ, 3 * W_STICK // 2, "plain"), (OFF_SQ + 3 * W_STICK // 2, 3 * W_STICK // 2, "plain")]
)


def _rope_lanes(t, cos, sin_up, sin_dn):
    return (t * cos + pltpu.roll(t, ROPE_DIM // 2, 1) * sin_up
            + pltpu.roll(t, LANES - ROPE_DIM // 2, 1) * sin_dn)


def _inproj_kernel(tiles_per_seq, x_ref, g_ref, w_ref, wf_ref, fb_ref, tri_ref, cos_ref, sup_ref,
                   sdn_ref, slab_ref, decay_ref, xn_ref, carry_ref):
    @pl.when(pl.program_id(0) % tiles_per_seq == 0)
    def _():
        carry_ref[...] = jnp.zeros(carry_ref.shape, F32)

    def project(chunks):
        for off, width, kind in chunks:
            res = _dot(xn_ref[...], w_ref[:, off:off + width])
            if kind == "rope":
                cos, sup, sdn = cos_ref[...], sup_ref[...], sdn_ref[...]
                for c in range(width // LANES):
                    blk = _rope_lanes(res[:, c * LANES:(c + 1) * LANES], cos, sup, sdn)
                    slab_ref[:, off + c * LANES:off + (c + 1) * LANES] = blk.astype(BF16)
            else:
                slab_ref[:, off:off + width] = res.astype(BF16)

    x = x_ref[...]
    xn_ref[...] = ((x * _rms_scale(x)) * g_ref[...]).astype(BF16)
    log_f = _log_sigmoid_parts(_dot(xn_ref[...], wf_ref[...]) + fb_ref[...])[0]
    project(_PROJ_CHUNKS[:3])
    tri = tri_ref[...]
    c = carry_ref[0:1, :]
    for part in _split_bf16(log_f, 3):
        c = c + _dot(tri, part)
    project(_PROJ_CHUNKS[3:])
    tm = c.shape[0]
    carry_ref[...] = jnp.broadcast_to(c[tm - 1:tm, :], carry_ref.shape)
    hi, mid, lo = _split_bf16(c * (-LOG2_E), 3)
    lane = lax.broadcasted_iota(jnp.int32, c.shape, 1)
    part = jnp.where(lane < H_FOX, hi, jnp.where(lane < 2 * H_FOX, mid, lo))
    decay_ref[...] = jnp.where(lane < 3 * H_FOX, part, jnp.zeros_like(part))


def _inproj(x2d, g, w_slab, w_f, f_bias, cos_t, sup_t, sdn_t, seq_len):
    n_tok = x2d.shape[0]
    tm = TOKEN_TILE
    tiles_per_seq = seq_len // tm
    const = lambda i: (0, 0)
    rows = lambda i: (i, 0)
    rope_map = lambda i: (i % tiles_per_seq, 0)
    tri = jnp.tril(jnp.ones((tm, tm), BF16))
    return pl.pallas_call(
        functools.partial(_inproj_kernel, tiles_per_seq),
        grid=(n_tok // tm,),
        in_specs=[
            pl.BlockSpec((tm, D_MODEL), rows),
            pl.BlockSpec((1, D_MODEL), const),
            pl.BlockSpec((D_MODEL, N_SLAB), const, pipeline_mode=pl.Buffered(1)),
            pl.BlockSpec((D_MODEL, LANES), const, pipeline_mode=pl.Buffered(1)),
            pl.BlockSpec((1, LANES), const),
            pl.BlockSpec((tm, tm), const, pipeline_mode=pl.Buffered(1)),
            pl.BlockSpec((tm, LANES), rope_map),
            pl.BlockSpec((tm, LANES), rope_map),
            pl.BlockSpec((tm, LANES), rope_map),
        ],
        out_specs=[
            pl.BlockSpec((tm, N_SLAB), rows),
            pl.BlockSpec((tm, LANES), rows),
        ],
        out_shape=[
            jax.ShapeDtypeStruct((n_tok, N_SLAB), BF16),
            jax.ShapeDtypeStruct((n_tok, LANES), BF16),
        ],
        scratch_shapes=[pltpu.VMEM((tm, D_MODEL), BF16), pltpu.VMEM((SUBLANES, LANES), F32)],
        compiler_params=pltpu.CompilerParams(
            dimension_semantics=("arbitrary",), vmem_limit_bytes=VMEM_LIMIT_BYTES),
        name="inproj",
    )(x2d, g, w_slab, w_f, f_bias, tri, cos_t, sup_t, sdn_t)


def _stack_query_halves(q_ref, qm_ref):
    t = ATTN_TILE
    lane = lax.broadcasted_iota(jnp.int32, (t, LANES), 1)
    for g in range(ATTN_GROUP):
        for r in range(Q_CHUNKS):
            q = q_ref[g, r * t:(r + 1) * t, :]
            zero = jnp.zeros_like(q)
            qm_ref[_half_rows(g, r, 0), 0:LANES] = jnp.where(lane < HEAD_DIM, q, zero)
            qm_ref[_half_rows(g, r, 1), 0:LANES] = jnp.where(lane >= HEAD_DIM, q, zero)


def _key_before_query(strict):
    query = lax.broadcasted_iota(jnp.int32, (ATTN_TILE, ATTN_TILE), 0)
    key = lax.broadcasted_iota(jnp.int32, (ATTN_TILE, ATTN_TILE), 1)
    return key < query if strict else key <= query


def _causal_key_tiles(scores_fn, update_fn, off_keys, descending):
    t = ATTN_TILE
    block_start = Q_CHUNKS * t * pl.program_id(2)
    n_off = pl.program_id(2) * (Q_CHUNKS * t // off_keys)

    def step(ks, chunks):
        update_fn(ks, chunks, scores_fn(ks, chunks))

    def off_diagonal(jj, carry):
        j = n_off - 1 - jj if descending else jj
        step(pl.ds(pl.multiple_of(j * off_keys, off_keys), off_keys), _visible_chunks(None))
        return carry

    def diagonal(d):
        step(pl.ds(pl.multiple_of(block_start + d * t, t), t), _visible_chunks(d))

    if descending:
        for d in reversed(range(Q_CHUNKS)):
            diagonal(d)
        lax.fori_loop(0, n_off, off_diagonal, 0)
    else:
        lax.fori_loop(0, n_off, off_diagonal, 0)
        for d in range(Q_CHUNKS):
            diagonal(d)


def _visible_chunks(diag_chunk):
    first = 0 if diag_chunk is None else diag_chunk
    return [(g, r, r == diag_chunk) for r in range(first, Q_CHUNKS) for g in range(ATTN_GROUP)]


def _half_rows(g, r, half):
    block = (g * Q_CHUNKS + r) * 2 + half
    return slice(block * ATTN_TILE, (block + 1) * ATTN_TILE)


def _chunk_scores(qm_ref, k_tiles, chunks):
    return [_dot_nt(qm_ref[_half_rows(g, r, 0).start:_half_rows(g, r, 1).stop, :], k_tiles[g])
            for g, r, _ in chunks]


def _softmax_update(s, on_diagonal, v_aug, rows, m_ref, acc_ref):
    if on_diagonal:
        s = jnp.where(_key_before_query(strict=False), s, -jnp.inf)
    m_prev = m_ref[rows, :]
    m_new = jnp.maximum(m_prev, jnp.max(s, axis=-1, keepdims=True))
    p = jnp.exp2(s - jnp.concatenate([m_new] * (s.shape[1] // LANES), axis=1)).astype(BF16)
    alpha = jnp.exp2(m_prev - m_new)
    if acc_ref.shape[1] != LANES:
        alpha = jnp.concatenate([alpha] * (acc_ref.shape[1] // LANES), axis=1)
    acc_ref[rows, :] = alpha * acc_ref[rows, :] + _dot(p, v_aug)
    m_ref[rows, :] = m_new


def _init_softmax_state(m_ref, acc_ref):
    m_ref[...] = jnp.full(m_ref.shape, -jnp.inf, F32)
    acc_ref[...] = jnp.zeros(acc_ref.shape, F32)


def _fox_kernel(q_ref, k_ref, v_ref, decay_ref, o_ref, qm_ref, m_ref, acc_ref):
    t = ATTN_TILE
    _stack_query_halves(q_ref, qm_ref)
    _init_softmax_state(m_ref, acc_ref)
    lane = lax.broadcasted_iota(jnp.int32, (t, LANES), 1)
    for half in range(2):
        head = 2 * pl.program_id(1) + half
        own = (lane == head) | (lane == head + H_FOX) | (lane == head + 2 * H_FOX)
        ones_in_own_lanes = jnp.where(own, 1.0, 0.0).astype(BF16)
        for g in range(ATTN_GROUP):
            for r in range(Q_CHUNKS):
                qm_ref[_half_rows(g, r, half), LANES:2 * LANES] = ones_in_own_lanes

    def scores(ks, chunks):
        k_aug = [jnp.concatenate([k_ref[g, ks, :], decay_ref[g, ks, :]], axis=1)
                 for g in range(ATTN_GROUP)]
        return _chunk_scores(qm_ref, k_aug, chunks)

    def update(ks, chunks, score_list):
        v_aug = []
        for g in range(ATTN_GROUP):
            v = v_ref[g, ks, :]
            one = jnp.ones_like(v)
            lane_k = lax.broadcasted_iota(jnp.int32, v.shape, 1)
            v_aug.append([jnp.where(lane_k < HEAD_DIM, v, one), jnp.where(lane_k >= HEAD_DIM, v, one)])
        for (g, r, on_diagonal), s in zip(chunks, score_list):
            for half in range(2):
                _softmax_update(s[half * t:(half + 1) * t], on_diagonal, v_aug[g][half],
                                _half_rows(g, r, half), m_ref, acc_ref)

    _causal_key_tiles(scores, update, SOFTMAX_OFF_KEYS, descending=False)
    for g in range(ATTN_GROUP):
        for r in range(Q_CHUNKS):
            a0 = acc_ref[_half_rows(g, r, 0), :]
            a1 = acc_ref[_half_rows(g, r, 1), :]
            num = jnp.where(lane < HEAD_DIM, a0, a1)
            den = jnp.where(lane < HEAD_DIM, pltpu.roll(a0, HEAD_DIM, 1), pltpu.roll(a1, HEAD_DIM, 1))
            o_ref[g, r * t:(r + 1) * t, :] = (num / den).astype(o_ref.dtype)


def _diff_kernel(lam_init, q_ref, k_ref, v_ref, lam_ref, g_ref, o_ref, qm_ref, m_ref, acc_ref):
    t = ATTN_TILE
    dv = 2 * HEAD_DIM
    _stack_query_halves(q_ref, qm_ref)
    _init_softmax_state(m_ref, acc_ref)

    def scores(ks, chunks):
        return _chunk_scores(qm_ref, [k_ref[g, ks, :] for g in range(ATTN_GROUP)], chunks)

    def update(ks, chunks, score_list):
        v_aug = []
        for g in range(ATTN_GROUP):
            v = v_ref[g, ks, :]
            v_aug.append(jnp.concatenate([v, jnp.ones_like(v)], axis=1))
        for (g, r, on_diagonal), s in zip(chunks, score_list):
            for half in range(2):
                _softmax_update(s[half * t:(half + 1) * t], on_diagonal, v_aug[g],
                                _half_rows(g, r, half), m_ref, acc_ref)

    _causal_key_tiles(scores, update, SOFTMAX_OFF_KEYS, descending=False)
    lp = lam_ref[...]
    lam = (jnp.exp(jnp.sum(lp[0:1] * lp[1:2], axis=-1, keepdims=True))
           - jnp.exp(jnp.sum(lp[2:3] * lp[3:4], axis=-1, keepdims=True)) + lam_init)
    for g in range(ATTN_GROUP):
        for r in range(Q_CHUNKS):
            a1 = acc_ref[_half_rows(g, r, 0), :]
            a2 = acc_ref[_half_rows(g, r, 1), :]
            o = a1[:, :dv] / a1[:, dv:] - lam * (a2[:, :dv] / a2[:, dv:])
            o = (o * _rms_scale(o)) * g_ref[...]
            o_ref[g, r * t:(r + 1) * t, :] = (o * (1.0 - lam_init)).astype(o_ref.dtype)


def _stick_kernel(q_ref, k_ref, v_ref, o_ref, qm_ref, suffix_ref, r_ref, acc_ref):
    t = ATTN_TILE
    _stack_query_halves(q_ref, qm_ref)
    r_ref[...] = jnp.zeros(r_ref.shape, F32)
    acc_ref[...] = jnp.zeros(acc_ref.shape, F32)
    st = SUFFIX_TILE
    row = lax.broadcasted_iota(jnp.int32, (2 * st, st), 0)
    col = lax.broadcasted_iota(jnp.int32, (2 * st, st), 1)
    suffix_ref[...] = jnp.where((row >= col) & ((row < st) | (row >= col + st)), -1.0, 0.0).astype(BF16)

    def scores(ks, chunks):
        return _chunk_scores(qm_ref, [k_ref[g, ks, :] for g in range(ATTN_GROUP)], chunks)

    def update(ks, chunks, logits):
        values = [v_ref[g, ks, :] for g in range(ATTN_GROUP)]
        suffix = suffix_ref[...]
        n_sub = values[0].shape[0] // st
        blocks = [(i, g, r, half, on_diagonal)
                  for i, (g, r, on_diagonal) in enumerate(chunks) for half in range(2)]
        z_blocks, sums = [], []
        for i, g, r, half, on_diagonal in blocks:
            z = logits[i][half * t:(half + 1) * t]
            neg_log_1m = jnp.maximum(z, 0.0) + jnp.log2(1.0 + jnp.exp2(-jnp.abs(z)))
            if on_diagonal:
                neg_log_1m = jnp.where(_key_before_query(strict=True), neg_log_1m, 0.0)
            z_blocks.append(z)
            sums.append([
                _dot(jnp.concatenate(_split_bf16(neg_log_1m[:, u * st:(u + 1) * st], 2), axis=1), suffix)
                for u in range(n_sub)])
        for (_, g, r, half, on_diagonal), z, s in zip(blocks, z_blocks, sums):
            rows = _half_rows(g, r, half)
            r_run = r_ref[rows, :]
            w = [None] * n_sub
            for u in reversed(range(n_sub)):
                w_u = jnp.exp2(z[:, u * st:(u + 1) * st] + s[u]
                               + jnp.concatenate([r_run] * (st // LANES), axis=1))
                if on_diagonal:
                    w_u = jnp.where(_key_before_query(strict=True)[:, u * st:(u + 1) * st], w_u, 0.0)
                w[u] = w_u.astype(BF16)
                r_run = r_run + jnp.broadcast_to(s[u][:, 0:1], (t, LANES))
            acc_ref[rows, :] += _dot(jnp.concatenate(w, axis=1), values[g])
            r_ref[rows, :] = r_run

    _causal_key_tiles(scores, update, STICK_OFF_KEYS, descending=True)
    lane = lax.broadcasted_iota(jnp.int32, (t, LANES), 1)
    for g in range(ATTN_GROUP):
        for r in range(Q_CHUNKS):
            a0 = acc_ref[_half_rows(g, r, 0), :]
            a1 = acc_ref[_half_rows(g, r, 1), :]
            o_ref[g, r * t:(r + 1) * t, :] = jnp.where(lane < HEAD_DIM, a0, a1).astype(o_ref.dtype)


def _attention_call(kernel, slab, extra_inputs, extra_specs, n_blocks, q_blk, k_blk, v_blk,
                    scratch, name, query_lanes=LANES):
    b, seq_len, _ = slab.shape
    t = ATTN_TILE
    tq = Q_CHUNKS * t
    grp = ATTN_GROUP
    q_spec = pl.BlockSpec((grp, tq, LANES), lambda bi, h, i: (bi, i, q_blk + h))
    k_spec = pl.BlockSpec((grp, seq_len, LANES), lambda bi, h, i: (bi, 0, k_blk + h))
    v_spec = pl.BlockSpec((grp, seq_len, LANES), lambda bi, h, i: (bi, 0, v_blk + h))
    return pl.pallas_call(
        kernel,
        grid=(b // grp, n_blocks, seq_len // tq),
        in_specs=[q_spec, k_spec, v_spec] + extra_specs,
        out_specs=pl.BlockSpec((grp, tq, LANES), lambda bi, h, i: (bi, i, h)),
        out_shape=jax.ShapeDtypeStruct((b, seq_len, n_blocks * LANES), BF16),
        scratch_shapes=[pltpu.VMEM((_STATE_ROWS, query_lanes), BF16)] + scratch,
        compiler_params=pltpu.CompilerParams(
            dimension_semantics=("arbitrary", "arbitrary", "arbitrary"),
            vmem_limit_bytes=VMEM_LIMIT_BYTES),
        name=name,
    )(slab, slab, slab, *extra_inputs)


def _softmax_scratch(acc_lanes):
    return [pltpu.VMEM((_STATE_ROWS, LANES), F32), pltpu.VMEM((_STATE_ROWS, acc_lanes), F32)]


def _fox_attention(slab, decay):
    decay_spec = pl.BlockSpec((ATTN_GROUP,) + decay.shape[1:], lambda bi, h, i: (bi, 0, 0))
    return _attention_call(
        _fox_kernel, slab, [decay], [decay_spec], H_FOX // 2,
        OFF_FQ // LANES, OFF_FK // LANES, OFF_FV // LANES, _softmax_scratch(LANES), "fox_attention",
        query_lanes=2 * LANES)


def _diff_attention(slab, lam_params, subln_g, lam_init):
    const = lambda bi, h, i: (0, 0)
    specs = [pl.BlockSpec(lam_params.shape, const), pl.BlockSpec(subln_g.shape, const)]
    return _attention_call(
        functools.partial(_diff_kernel, lam_init), slab, [lam_params, subln_g], specs, H_DIFF,
        OFF_DQ // LANES, OFF_DK // LANES, OFF_DV // LANES, _softmax_scratch(2 * LANES),
        "diff_attention")


def _stick_attention(slab):
    st = SUFFIX_TILE
    scratch = [pltpu.VMEM((2 * st, st), BF16), pltpu.VMEM((_STATE_ROWS, LANES), F32),
               pltpu.VMEM((_STATE_ROWS, LANES), F32)]
    return _attention_call(
        _stick_kernel, slab, [], [], H_STICK // 2,
        OFF_SQ // LANES, OFF_SK // LANES, OFF_SV // LANES, scratch, "stick_attention")


def _shift_rows(u, prev, n):
    rolled = pltpu.roll(u, n, 0)
    head = rolled[:SUBLANES]
    row = lax.broadcasted_iota(jnp.int32, head.shape, 0)
    for r in range(n):
        src = SUBLANES - n + r
        head = jnp.where(row == r, prev[src:src + 1], head)
    return jnp.concatenate([head, rolled[SUBLANES:]], axis=0)


def _mix_ffn_kernel(tiles_per_seq, final, x_ref, ga_ref, wg_ref, of_ref, od_ref, os_ref, wf_ref,
                    wd_ref, ws_ref, wo_ref, g_ref, wup_ref, cw_ref, cb_ref, wdn_ref, gfin_ref,
                    y_ref, hn_ref, act_ref, tail_ref):
    tm = x_ref.shape[0]
    d = D_MODEL

    @pl.when(pl.program_id(0) % tiles_per_seq == 0)
    def _():
        tail_ref[...] = jnp.zeros(tail_ref.shape, F32)

    x = x_ref[...]
    hn_ref[...] = ((x * _rms_scale(x)) * ga_ref[...]).astype(BF16)
    mixed = None
    for branch, (o_ref, w_ref) in enumerate(((of_ref, wf_ref), (od_ref, wd_ref), (os_ref, ws_ref))):
        gate = jax.nn.sigmoid(_dot(hn_ref[...], wg_ref[:, branch * d:(branch + 1) * d]))
        term = gate * _dot(o_ref[...], w_ref[...])
        mixed = term if mixed is None else mixed + term
    x = x + _dot(mixed.astype(BF16), wo_ref[...])

    hn_ref[...] = ((x * _rms_scale(x)) * g_ref[...]).astype(BF16)
    for c in range(D_FF // FF_CHUNK):
        conv = []
        for part in range(2):
            cols = slice(part * D_FF + c * FF_CHUNK, part * D_FF + (c + 1) * FF_CHUNK)
            u = _dot(hn_ref[...], wup_ref[:, cols])
            prev = tail_ref[:, cols]
            tail_ref[:, cols] = u[tm - SUBLANES:]
            conv.append(cw_ref[0:1, cols] * _shift_rows(u, prev, 2)
                        + cw_ref[1:2, cols] * _shift_rows(u, prev, 1)
                        + cw_ref[2:3, cols] * u + cb_ref[:, cols])
        gate, val = conv
        act_ref[:, c * FF_CHUNK:(c + 1) * FF_CHUNK] = (gate * jax.nn.sigmoid(gate) * val).astype(BF16)
    y = x + _dot(act_ref[...], wdn_ref[...])
    if final:
        y = (y * _rms_scale(y)) * gfin_ref[...]
    y_ref[...] = y


def _mix_ffn(x2d, g_attn, w_gate, o_fox, o_diff, o_stick, w_bf, w_bd, w_bs, w_out,
             g, w_up, conv_w, conv_b, w_down, g_final, seq_len, final):
    n_tok = x2d.shape[0]
    tm = TOKEN_TILE
    const = lambda i: (0, 0)
    rows = lambda i: (i, 0)
    resident = lambda w: pl.BlockSpec(w.shape, const, pipeline_mode=pl.Buffered(1))
    return pl.pallas_call(
        functools.partial(_mix_ffn_kernel, seq_len // tm, final),
        grid=(n_tok // tm,),
        in_specs=[
            pl.BlockSpec((tm, D_MODEL), rows),
            pl.BlockSpec((1, D_MODEL), const),
            resident(w_gate),
            pl.BlockSpec((tm, W_FOX), rows),
            pl.BlockSpec((tm, W_DIFF), rows),
            pl.BlockSpec((tm, W_STICK), rows),
            resident(w_bf), resident(w_bd), resident(w_bs), resident(w_out),
            pl.BlockSpec((1, D_MODEL), const),
            resident(w_up),
            pl.BlockSpec(conv_w.shape, const),
            pl.BlockSpec(conv_b.shape, const),
            resident(w_down),
            pl.BlockSpec((1, D_MODEL), const),
        ],
        out_specs=pl.BlockSpec((tm, D_MODEL), rows),
        out_shape=jax.ShapeDtypeStruct((n_tok, D_MODEL), F32),
        scratch_shapes=[
            pltpu.VMEM((tm, D_MODEL), BF16),
            pltpu.VMEM((tm, D_FF), BF16),
            pltpu.VMEM((SUBLANES, 2 * D_FF), F32),
        ],
        compiler_params=pltpu.CompilerParams(
            dimension_semantics=("arbitrary",), vmem_limit_bytes=VMEM_LIMIT_BYTES),
        name="mix_ffn",
    )(x2d, g_attn, w_gate, o_fox, o_diff, o_stick, w_bf, w_bd, w_bs, w_out,
      g, w_up, conv_w, conv_b, w_down, g_final)


def _rope_tables(seq_len):
    half = ROPE_DIM // 2
    pos = jnp.arange(seq_len, dtype=F32)
    inv_freq = ROPE_THETA ** (-jnp.arange(0, ROPE_DIM, 2, dtype=F32) / ROPE_DIM)
    ang = pos[:, None] * inv_freq[None, :]
    cos, sin = jnp.cos(ang), jnp.sin(ang)
    ones = jnp.ones((seq_len, HEAD_DIM - ROPE_DIM), F32)
    zeros = jnp.zeros((seq_len, HEAD_DIM - ROPE_DIM), F32)
    zh = jnp.zeros((seq_len, half), F32)
    cos_h = jnp.concatenate([cos, cos, ones], axis=-1)
    sup_h = jnp.concatenate([zh, sin, zeros], axis=-1)
    sdn_h = jnp.concatenate([-sin, zh, zeros], axis=-1)
    rep = LANES // HEAD_DIM
    return tuple(jnp.tile(t, (1, rep)) for t in (cos_h, sup_h, sdn_h))


def _slab_weight(w_in_l):
    scale = jnp.ones((N_SLAB,), F32)
    for off, width in ((OFF_FQ, W_FOX), (OFF_DQ, W_DIFF_QK), (OFF_SQ, W_STICK)):
        scale = scale.at[off:off + width].set(QK_SCALE * LOG2_E)
    return (w_in_l[:, :N_SLAB] * scale[None, :]).astype(BF16)


def _forget_weight(w_in_l, forget_bias_l):
    w_f = jnp.tile(w_in_l[:, IN_OFF_FORGET:IN_OFF_FORGET + H_FOX], (1, 3))
    w_f = jnp.pad(w_f, ((0, 0), (0, LANES - 3 * H_FOX))).astype(BF16)
    bias = jnp.pad(jnp.tile(forget_bias_l.astype(F32), 3), (0, LANES - 3 * H_FOX)).reshape(1, LANES)
    return w_f, bias


def kernel(x, attn_norm_g, w_in, forget_bias, lam_q1, lam_k1, lam_q2, lam_k2, diff_subln_g,
           w_br_fox, w_br_diff, w_br_stick, w_out, ffn_norm_g, w_up, conv_w, conv_b, w_down,
           final_norm_g):
    b, seq_len, d = x.shape
    depth = w_in.shape[0]
    n_tok = b * seq_len
    cos_t, sup_t, sdn_t = _rope_tables(seq_len)
    x2d = x.reshape(n_tok, d)
    g_final = final_norm_g.reshape(1, d)

    for l in range(depth):
        lam_init = 0.8 - 0.6 * math.exp(-0.3 * l)
        w_f, f_bias = _forget_weight(w_in[l], forget_bias[l])
        slab2d, decay = _inproj(x2d, attn_norm_g[l].reshape(1, d), _slab_weight(w_in[l]), w_f, f_bias,
                                cos_t, sup_t, sdn_t, seq_len)
        slab = slab2d.reshape(b, seq_len, N_SLAB)

        o_fox = _fox_attention(slab, decay.reshape(b, seq_len, LANES))
        lam_params = jnp.stack([lam_q1[l], lam_k1[l], lam_q2[l], lam_k2[l]]).astype(F32)
        o_diff = _diff_attention(slab, lam_params, diff_subln_g[l].reshape(1, -1), lam_init)
        o_stick = _stick_attention(slab)

        x2d = _mix_ffn(x2d, attn_norm_g[l].reshape(1, d), w_in[l][:, IN_OFF_GATE:].astype(BF16),
                       o_fox.reshape(n_tok, -1), o_diff.reshape(n_tok, -1), o_stick.reshape(n_tok, -1),
                       w_br_fox[l].astype(BF16), w_br_diff[l].astype(BF16), w_br_stick[l].astype(BF16),
                       w_out[l].astype(BF16), ffn_norm_g[l].reshape(1, d), w_up[l].astype(BF16),
                       conv_w[l], conv_b[l].reshape(1, -1), w_down[l].astype(BF16), g_final,
                       seq_len, final=(l == depth - 1))
    return x2d.reshape(b, seq_len, d)
```

```python
import functools
import math

import jax
import jax.numpy as jnp
from jax import lax
from jax.experimental import pallas as pl
from jax.experimental.pallas import tpu as pltpu

F32 = jnp.float32
BF16 = jnp.bfloat16

D_MODEL = 1024
HEAD_DIM = 64
H_FOX = 4
H_DIFF = 4
H_STICK = 4
W_FOX = H_FOX * HEAD_DIM
W_DIFF_QK = 2 * H_DIFF * HEAD_DIM
W_DIFF = H_DIFF * 2 * HEAD_DIM
W_STICK = H_STICK * HEAD_DIM
N_BRANCH = 3
ROPE_THETA = 500000.0
ROPE_DIM = HEAD_DIM // 4
D_FF = 2816
CONV_WIDTH = 3
NORM_EPS = 1e-6
QK_SCALE = HEAD_DIM ** -0.5
LOG2_E = math.log2(math.e)

LANES = 128
SUBLANES = 8
VMEM_LIMIT_BYTES = 56 * 1024 * 1024

OFF_FQ = 0
OFF_FK = OFF_FQ + W_FOX
OFF_FV = OFF_FK + W_FOX
OFF_DQ = OFF_FV + W_FOX
OFF_DK = OFF_DQ + W_DIFF_QK
OFF_DV = OFF_DK + W_DIFF_QK
OFF_SQ = OFF_DV + W_DIFF
OFF_SK = OFF_SQ + W_STICK
OFF_SV = OFF_SK + W_STICK
N_SLAB = OFF_SV + W_STICK
IN_OFF_FORGET = N_SLAB
IN_OFF_GATE = IN_OFF_FORGET + H_FOX

TOKEN_TILE = 512
ATTN_TILE = 256
SUFFIX_TILE = 256
Q_CHUNKS = 4
ATTN_GROUP = 1
_STATE_ROWS = ATTN_GROUP * Q_CHUNKS * 2 * ATTN_TILE
SOFTMAX_OFF_KEYS = 1024
STICK_OFF_KEYS = 512
FF_CHUNK = 256


def _log_sigmoid_parts(z):
    l1p = jnp.log(1.0 + jnp.exp(-jnp.abs(z)))
    return jnp.minimum(z, 0.0) - l1p, -jnp.maximum(z, 0.0) - l1p


def _rms_scale(x):
    return lax.rsqrt(jnp.mean(x * x, axis=-1, keepdims=True) + NORM_EPS)


def _split_bf16(v, n_parts):
    parts = []
    rem = v
    for _ in range(n_parts):
        p = rem.astype(BF16)
        parts.append(p)
        rem = rem - p.astype(F32)
    return parts


def _dot_nt(a, b):
    return lax.dot_general(a, b, (((1,), (1,)), ((), ())), preferred_element_type=F32)


def _dot(a, b):
    return jnp.dot(a, b, preferred_element_type=F32)


_PROJ_CHUNKS = (
    [(OFF_FQ, 3 * W_FOX // 2, "plain"), (OFF_FQ + 3 * W_FOX // 2, 3 * W_FOX // 2, "plain")]
    + [(OFF_DQ, W_DIFF_QK, "rope"), (OFF_DK, W_DIFF_QK, "rope"), (OFF_DV, W_DIFF, "plain")]
    + [(OFF_SQ, 3 * W_STICK // 2, "plain"), (OFF_SQ + 3 * W_STICK // 2, 3 * W_STICK // 2, "plain")]
)


def _rope_lanes(t, cos, sin_up, sin_dn):
    return (t * cos + pltpu.roll(t, ROPE_DIM // 2, 1) * sin_up
            + pltpu.roll(t, LANES - ROPE_DIM // 2, 1) * sin_dn)


def _inproj_kernel(tiles_per_seq, x_ref, g_ref, w_ref, scale_ref, wf_ref, fb_ref, tri_ref, cos_ref,
                   sup_ref, sdn_ref, slab_ref, decay_ref, xn_ref, carry_ref):
    @pl.when(pl.program_id(0) % tiles_per_seq == 0)
    def _():
        carry_ref[...] = jnp.zeros(carry_ref.shape, F32)

    def project(chunks):
        for off, width, kind in chunks:
            res = _dot(xn_ref[...], w_ref[:, off:off + width]) * scale_ref[:, off:off + width]
            if kind == "rope":
                cos, sup, sdn = cos_ref[...], sup_ref[...], sdn_ref[...]
                for c in range(width // LANES):
                    blk = _rope_lanes(res[:, c * LANES:(c + 1) * LANES], cos, sup, sdn)
                    slab_ref[:, off + c * LANES:off + (c + 1) * LANES] = blk.astype(BF16)
            else:
                slab_ref[:, off:off + width] = res.astype(BF16)

    x = x_ref[...]
    xn_ref[...] = ((x * _rms_scale(x)) * g_ref[...]).astype(BF16)
    log_f = _log_sigmoid_parts(_dot(xn_ref[...], wf_ref[...]) + fb_ref[...])[0]
    project(_PROJ_CHUNKS[:3])
    tri = tri_ref[...]
    c = carry_ref[0:1, :]
    for part in _split_bf16(log_f, 3):
        c = c + _dot(tri, part)
    project(_PROJ_CHUNKS[3:])
    tm = c.shape[0]
    carry_ref[...] = jnp.broadcast_to(c[tm - 1:tm, :], carry_ref.shape)
    hi, mid, lo = _split_bf16(c * (-LOG2_E), 3)
    lane = lax.broadcasted_iota(jnp.int32, c.shape, 1)
    part = jnp.where(lane < H_FOX, hi, jnp.where(lane < 2 * H_FOX, mid, lo))
    decay_ref[...] = jnp.where(lane < 3 * H_FOX, part, jnp.zeros_like(part))


def _layer_resident(stacked, layer):
    return pl.BlockSpec((None,) + stacked.shape[1:], lambda i: (layer, 0, 0),
                        pipeline_mode=pl.Buffered(1))


def _inproj(x2d, g, w_in_b, q_scale, w_f, f_bias, cos_t, sup_t, sdn_t, seq_len, layer):
    n_tok = x2d.shape[0]
    tm = TOKEN_TILE
    tiles_per_seq = seq_len // tm
    const = lambda i: (0, 0)
    rows = lambda i: (i, 0)
    rope_map = lambda i: (i % tiles_per_seq, 0)
    tri = jnp.tril(jnp.ones((tm, tm), BF16))
    return pl.pallas_call(
        functools.partial(_inproj_kernel, tiles_per_seq),
        grid=(n_tok // tm,),
        in_specs=[
            pl.BlockSpec((tm, D_MODEL), rows),
            pl.BlockSpec((1, D_MODEL), const),
            pl.BlockSpec((None, D_MODEL, N_SLAB), lambda i: (layer, 0, 0), pipeline_mode=pl.Buffered(1)),
            pl.BlockSpec((1, N_SLAB), const),
            _layer_resident(w_f, layer),
            pl.BlockSpec((1, LANES), const),
            pl.BlockSpec((tm, tm), const, pipeline_mode=pl.Buffered(1)),
            pl.BlockSpec((tm, LANES), rope_map),
            pl.BlockSpec((tm, LANES), rope_map),
            pl.BlockSpec((tm, LANES), rope_map),
        ],
        out_specs=[
            pl.BlockSpec((tm, N_SLAB), rows),
            pl.BlockSpec((tm, LANES), rows),
        ],
        out_shape=[
            jax.ShapeDtypeStruct((n_tok, N_SLAB), BF16),
            jax.ShapeDtypeStruct((n_tok, LANES), BF16),
        ],
        scratch_shapes=[pltpu.VMEM((tm, D_MODEL), BF16), pltpu.VMEM((SUBLANES, LANES), F32)],
        compiler_params=pltpu.CompilerParams(
            dimension_semantics=("arbitrary",), vmem_limit_bytes=VMEM_LIMIT_BYTES),
        name="inproj",
    )(x2d, g, w_in_b, q_scale, w_f, f_bias, tri, cos_t, sup_t, sdn_t)


def _stack_query_halves(q_ref, qm_ref):
    t = ATTN_TILE
    lane = lax.broadcasted_iota(jnp.int32, (t, LANES), 1)
    for g in range(ATTN_GROUP):
        for r in range(Q_CHUNKS):
            q = q_ref[g, r * t:(r + 1) * t, :]
            zero = jnp.zeros_like(q)
            qm_ref[_half_rows(g, r, 0), 0:LANES] = jnp.where(lane < HEAD_DIM, q, zero)
            qm_ref[_half_rows(g, r, 1), 0:LANES] = jnp.where(lane >= HEAD_DIM, q, zero)


def _key_before_query(strict):
    query = lax.broadcasted_iota(jnp.int32, (ATTN_TILE, ATTN_TILE), 0)
    key = lax.broadcasted_iota(jnp.int32, (ATTN_TILE, ATTN_TILE), 1)
    return key < query if strict else key <= query


def _causal_key_tiles(scores_fn, update_fn, off_keys, descending):
    t = ATTN_TILE
    block_start = Q_CHUNKS * t * pl.program_id(2)
    n_off = pl.program_id(2) * (Q_CHUNKS * t // off_keys)

    def step(ks, chunks):
        update_fn(ks, chunks, scores_fn(ks, chunks))

    def off_diagonal(jj, carry):
        j = n_off - 1 - jj if descending else jj
        step(pl.ds(pl.multiple_of(j * off_keys, off_keys), off_keys), _visible_chunks(None))
        return carry

    def diagonal(d):
        step(pl.ds(pl.multiple_of(block_start + d * t, t), t), _visible_chunks(d))

    if descending:
        for d in reversed(range(Q_CHUNKS)):
            diagonal(d)
        lax.fori_loop(0, n_off, off_diagonal, 0)
    else:
        lax.fori_loop(0, n_off, off_diagonal, 0)
        for d in range(Q_CHUNKS):
            diagonal(d)


def _visible_chunks(diag_chunk):
    first = 0 if diag_chunk is None else diag_chunk
    return [(g, r, r == diag_chunk) for r in range(first, Q_CHUNKS) for g in range(ATTN_GROUP)]


def _half_rows(g, r, half):
    block = (g * Q_CHUNKS + r) * 2 + half
    return slice(block * ATTN_TILE, (block + 1) * ATTN_TILE)


def _chunk_scores(qm_ref, k_tiles, chunks):
    return [_dot_nt(qm_ref[_half_rows(g, r, 0).start:_half_rows(g, r, 1).stop, :], k_tiles[g])
            for g, r, _ in chunks]


def _softmax_update(s, on_diagonal, v_aug, rows, m_ref, acc_ref):
    if on_diagonal:
        s = jnp.where(_key_before_query(strict=False), s, -jnp.inf)
    m_prev = m_ref[rows, :]
    m_new = jnp.maximum(m_prev, jnp.max(s, axis=-1, keepdims=True))
    p = jnp.exp2(s - jnp.concatenate([m_new] * (s.shape[1] // LANES), axis=1)).astype(BF16)
    alpha = jnp.exp2(m_prev - m_new)
    if acc_ref.shape[1] != LANES:
        alpha = jnp.concatenate([alpha] * (acc_ref.shape[1] // LANES), axis=1)
    acc_ref[rows, :] = alpha * acc_ref[rows, :] + _dot(p, v_aug)
    m_ref[rows, :] = m_new


def _init_softmax_state(m_ref, acc_ref):
    m_ref[...] = jnp.full(m_ref.shape, -jnp.inf, F32)
    acc_ref[...] = jnp.zeros(acc_ref.shape, F32)


def _fox_kernel(q_ref, k_ref, v_ref, decay_ref, o_ref, qm_ref, m_ref, acc_ref):
    t = ATTN_TILE
    _stack_query_halves(q_ref, qm_ref)
    _init_softmax_state(m_ref, acc_ref)
    lane = lax.broadcasted_iota(jnp.int32, (t, LANES), 1)
    for half in range(2):
        head = 2 * pl.program_id(1) + half
        own = (lane == head) | (lane == head + H_FOX) | (lane == head + 2 * H_FOX)
        ones_in_own_lanes = jnp.where(own, 1.0, 0.0).astype(BF16)
        for g in range(ATTN_GROUP):
            for r in range(Q_CHUNKS):
                qm_ref[_half_rows(g, r, half), LANES:2 * LANES] = ones_in_own_lanes

    def scores(ks, chunks):
        k_aug = [jnp.concatenate([k_ref[g, ks, :], decay_ref[g, ks, :]], axis=1)
                 for g in range(ATTN_GROUP)]
        return _chunk_scores(qm_ref, k_aug, chunks)

    def update(ks, chunks, score_list):
        v_aug = []
        for g in range(ATTN_GROUP):
            v = v_ref[g, ks, :]
            one = jnp.ones_like(v)
            lane_k = lax.broadcasted_iota(jnp.int32, v.shape, 1)
            v_aug.append([jnp.where(lane_k < HEAD_DIM, v, one), jnp.where(lane_k >= HEAD_DIM, v, one)])
        for (g, r, on_diagonal), s in zip(chunks, score_list):
            for half in range(2):
                _softmax_update(s[half * t:(half + 1) * t], on_diagonal, v_aug[g][half],
                                _half_rows(g, r, half), m_ref, acc_ref)

    _causal_key_tiles(scores, update, SOFTMAX_OFF_KEYS, descending=False)
    for g in range(ATTN_GROUP):
        for r in range(Q_CHUNKS):
            a0 = acc_ref[_half_rows(g, r, 0), :]
            a1 = acc_ref[_half_rows(g, r, 1), :]
            num = jnp.where(lane < HEAD_DIM, a0, a1)
            den = jnp.where(lane < HEAD_DIM, pltpu.roll(a0, HEAD_DIM, 1), pltpu.roll(a1, HEAD_DIM, 1))
            o_ref[g, r * t:(r + 1) * t, :] = (num / den).astype(o_ref.dtype)


def _diff_kernel(lam_init, q_ref, k_ref, v_ref, lam_ref, g_ref, o_ref, qm_ref, m_ref, acc_ref):
    t = ATTN_TILE
    dv = 2 * HEAD_DIM
    _stack_query_halves(q_ref, qm_ref)
    _init_softmax_state(m_ref, acc_ref)

    def scores(ks, chunks):
        return _chunk_scores(qm_ref, [k_ref[g, ks, :] for g in range(ATTN_GROUP)], chunks)

    def update(ks, chunks, score_list):
        v_aug = []
        for g in range(ATTN_GROUP):
            v = v_ref[g, ks, :]
            v_aug.append(jnp.concatenate([v, jnp.ones_like(v)], axis=1))
        for (g, r, on_diagonal), s in zip(chunks, score_list):
            for half in range(2):
                _softmax_update(s[half * t:(half + 1) * t], on_diagonal, v_aug[g],
                                _half_rows(g, r, half), m_ref, acc_ref)

    _causal_key_tiles(scores, update, SOFTMAX_OFF_KEYS, descending=False)
    lp = lam_ref[...]
    lam = (jnp.exp(jnp.sum(lp[0:1] * lp[1:2], axis=-1, keepdims=True))
           - jnp.exp(jnp.sum(lp[2:3] * lp[3:4], axis=-1, keepdims=True)) + lam_init)
    for g in range(ATTN_GROUP):
        for r in range(Q_CHUNKS):
            a1 = acc_ref[_half_rows(g, r, 0), :]
            a2 = acc_ref[_half_rows(g, r, 1), :]
            o = a1[:, :dv] / a1[:, dv:] - lam * (a2[:, :dv] / a2[:, dv:])
            o = (o * _rms_scale(o)) * g_ref[...]
            o_ref[g, r * t:(r + 1) * t, :] = (o * (1.0 - lam_init)).astype(o_ref.dtype)


def _stick_kernel(q_ref, k_ref, v_ref, o_ref, qm_ref, suffix_ref, r_ref, acc_ref):
    t = ATTN_TILE
    _stack_query_halves(q_ref, qm_ref)
    r_ref[...] = jnp.zeros(r_ref.shape, F32)
    acc_ref[...] = jnp.zeros(acc_ref.shape, F32)
    st = SUFFIX_TILE
    row = lax.broadcasted_iota(jnp.int32, (2 * st, st), 0)
    col = lax.broadcasted_iota(jnp.int32, (2 * st, st), 1)
    suffix_ref[...] = jnp.where((row >= col) & ((row < st) | (row >= col + st)), -1.0, 0.0).astype(BF16)

    def scores(ks, chunks):
        return _chunk_scores(qm_ref, [k_ref[g, ks, :] for g in range(ATTN_GROUP)], chunks)

    def update(ks, chunks, logits):
        values = [v_ref[g, ks, :] for g in range(ATTN_GROUP)]
        suffix = suffix_ref[...]
        n_sub = values[0].shape[0] // st
        blocks = [(i, g, r, half, on_diagonal)
                  for i, (g, r, on_diagonal) in enumerate(chunks) for half in range(2)]
        z_blocks, sums = [], []
        for i, g, r, half, on_diagonal in blocks:
            z = logits[i][half * t:(half + 1) * t]
            neg_log_1m = jnp.maximum(z, 0.0) + jnp.log2(1.0 + jnp.exp2(-jnp.abs(z)))
            if on_diagonal:
                neg_log_1m = jnp.where(_key_before_query(strict=True), neg_log_1m, 0.0)
            z_blocks.append(z)
            sums.append([
                _dot(jnp.concatenate(_split_bf16(neg_log_1m[:, u * st:(u + 1) * st], 2), axis=1), suffix)
                for u in range(n_sub)])
        for (_, g, r, half, on_diagonal), z, s in zip(blocks, z_blocks, sums):
            rows = _half_rows(g, r, half)
            r_run = r_ref[rows, :]
            w = [None] * n_sub
            for u in reversed(range(n_sub)):
                w_u = jnp.exp2(z[:, u * st:(u + 1) * st] + s[u]
                               + jnp.concatenate([r_run] * (st // LANES), axis=1))
                if on_diagonal:
                    w_u = jnp.where(_key_before_query(strict=True)[:, u * st:(u + 1) * st], w_u, 0.0)
                w[u] = w_u.astype(BF16)
                r_run = r_run + jnp.broadcast_to(s[u][:, 0:1], (t, LANES))
            acc_ref[rows, :] += _dot(jnp.concatenate(w, axis=1), values[g])
            r_ref[rows, :] = r_run

    _causal_key_tiles(scores, update, STICK_OFF_KEYS, descending=True)
    lane = lax.broadcasted_iota(jnp.int32, (t, LANES), 1)
    for g in range(ATTN_GROUP):
        for r in range(Q_CHUNKS):
            a0 = acc_ref[_half_rows(g, r, 0), :]
            a1 = acc_ref[_half_rows(g, r, 1), :]
            o_ref[g, r * t:(r + 1) * t, :] = jnp.where(lane < HEAD_DIM, a0, a1).astype(o_ref.dtype)


def _attention_call(kernel, slab, extra_inputs, extra_specs, n_blocks, q_blk, k_blk, v_blk,
                    scratch, name, query_lanes=LANES):
    b, seq_len, _ = slab.shape
    t = ATTN_TILE
    tq = Q_CHUNKS * t
    grp = ATTN_GROUP
    q_spec = pl.BlockSpec((grp, tq, LANES), lambda bi, h, i: (bi, i, q_blk + h))
    k_spec = pl.BlockSpec((grp, seq_len, LANES), lambda bi, h, i: (bi, 0, k_blk + h))
    v_spec = pl.BlockSpec((grp, seq_len, LANES), lambda bi, h, i: (bi, 0, v_blk + h))
    return pl.pallas_call(
        kernel,
        grid=(b // grp, n_blocks, seq_len // tq),
        in_specs=[q_spec, k_spec, v_spec] + extra_specs,
        out_specs=pl.BlockSpec((grp, tq, LANES), lambda bi, h, i: (bi, i, h)),
        out_shape=jax.ShapeDtypeStruct((b, seq_len, n_blocks * LANES), BF16),
        scratch_shapes=[pltpu.VMEM((_STATE_ROWS, query_lanes), BF16)] + scratch,
        compiler_params=pltpu.CompilerParams(
            dimension_semantics=("arbitrary", "arbitrary", "arbitrary"),
            vmem_limit_bytes=VMEM_LIMIT_BYTES),
        name=name,
    )(slab, slab, slab, *extra_inputs)


def _softmax_scratch(acc_lanes):
    return [pltpu.VMEM((_STATE_ROWS, LANES), F32), pltpu.VMEM((_STATE_ROWS, acc_lanes), F32)]


def _fox_attention(slab, decay):
    decay_spec = pl.BlockSpec((ATTN_GROUP,) + decay.shape[1:], lambda bi, h, i: (bi, 0, 0))
    return _attention_call(
        _fox_kernel, slab, [decay], [decay_spec], H_FOX // 2,
        OFF_FQ // LANES, OFF_FK // LANES, OFF_FV // LANES, _softmax_scratch(LANES), "fox_attention",
        query_lanes=2 * LANES)


def _diff_attention(slab, lam_params, subln_g, lam_init):
    const = lambda bi, h, i: (0, 0)
    specs = [pl.BlockSpec(lam_params.shape, const), pl.BlockSpec(subln_g.shape, const)]
    return _attention_call(
        functools.partial(_diff_kernel, lam_init), slab, [lam_params, subln_g], specs, H_DIFF,
        OFF_DQ // LANES, OFF_DK // LANES, OFF_DV // LANES, _softmax_scratch(2 * LANES),
        "diff_attention")


def _stick_attention(slab):
    st = SUFFIX_TILE
    scratch = [pltpu.VMEM((2 * st, st), BF16), pltpu.VMEM((_STATE_ROWS, LANES), F32),
               pltpu.VMEM((_STATE_ROWS, LANES), F32)]
    return _attention_call(
        _stick_kernel, slab, [], [], H_STICK // 2,
        OFF_SQ // LANES, OFF_SK // LANES, OFF_SV // LANES, scratch, "stick_attention")


def _shift_rows(u, prev, n):
    rolled = pltpu.roll(u, n, 0)
    head = rolled[:SUBLANES]
    row = lax.broadcasted_iota(jnp.int32, head.shape, 0)
    for r in range(n):
        src = SUBLANES - n + r
        head = jnp.where(row == r, prev[src:src + 1], head)
    return jnp.concatenate([head, rolled[SUBLANES:]], axis=0)


def _mix_ffn_kernel(tiles_per_seq, final, x_ref, ga_ref, wg_ref, of_ref, od_ref, os_ref, wf_ref,
                    wd_ref, ws_ref, wo_ref, g_ref, wup_ref, cw_ref, cb_ref, wdn_ref, gfin_ref,
                    y_ref, hn_ref, act_ref, tail_ref):
    tm = x_ref.shape[0]
    d = D_MODEL

    @pl.when(pl.program_id(0) % tiles_per_seq == 0)
    def _():
        tail_ref[...] = jnp.zeros(tail_ref.shape, F32)

    x = x_ref[...]
    hn_ref[...] = ((x * _rms_scale(x)) * ga_ref[...]).astype(BF16)
    mixed = None
    for branch, (o_ref, w_ref) in enumerate(((of_ref, wf_ref), (od_ref, wd_ref), (os_ref, ws_ref))):
        gate = jax.nn.sigmoid(_dot(hn_ref[...], wg_ref[:, branch * d:(branch + 1) * d]))
        term = gate * _dot(o_ref[...], w_ref[...])
        mixed = term if mixed is None else mixed + term
    x = x + _dot(mixed.astype(BF16), wo_ref[...])

    hn_ref[...] = ((x * _rms_scale(x)) * g_ref[...]).astype(BF16)
    for c in range(D_FF // FF_CHUNK):
        conv = []
        for part in range(2):
            cols = slice(part * D_FF + c * FF_CHUNK, part * D_FF + (c + 1) * FF_CHUNK)
            u = _dot(hn_ref[...], wup_ref[:, cols])
            prev = tail_ref[:, cols]
            tail_ref[:, cols] = u[tm - SUBLANES:]
            conv.append(cw_ref[0:1, cols] * _shift_rows(u, prev, 2)
                        + cw_ref[1:2, cols] * _shift_rows(u, prev, 1)
                        + cw_ref[2:3, cols] * u + cb_ref[:, cols])
        gate, val = conv
        act_ref[:, c * FF_CHUNK:(c + 1) * FF_CHUNK] = (gate * jax.nn.sigmoid(gate) * val).astype(BF16)
    y = x + _dot(act_ref[...], wdn_ref[...])
    if final:
        y = (y * _rms_scale(y)) * gfin_ref[...]
    y_ref[...] = y


def _mix_ffn(x2d, g_attn, w_gate, o_fox, o_diff, o_stick, w_bf, w_bd, w_bs, w_out,
             g, w_up, conv_w, conv_b, w_down, g_final, seq_len, layer, final):
    n_tok = x2d.shape[0]
    tm = TOKEN_TILE
    const = lambda i: (0, 0)
    rows = lambda i: (i, 0)
    resident = lambda w: _layer_resident(w, layer)
    return pl.pallas_call(
        functools.partial(_mix_ffn_kernel, seq_len // tm, final),
        grid=(n_tok // tm,),
        in_specs=[
            pl.BlockSpec((tm, D_MODEL), rows),
            pl.BlockSpec((1, D_MODEL), const),
            resident(w_gate),
            pl.BlockSpec((tm, W_FOX), rows),
            pl.BlockSpec((tm, W_DIFF), rows),
            pl.BlockSpec((tm, W_STICK), rows),
            resident(w_bf), resident(w_bd), resident(w_bs), resident(w_out),
            pl.BlockSpec((1, D_MODEL), const),
            resident(w_up),
            pl.BlockSpec(conv_w.shape, const),
            pl.BlockSpec(conv_b.shape, const),
            resident(w_down),
            pl.BlockSpec((1, D_MODEL), const),
        ],
        out_specs=pl.BlockSpec((tm, D_MODEL), rows),
        out_shape=jax.ShapeDtypeStruct((n_tok, D_MODEL), F32),
        scratch_shapes=[
            pltpu.VMEM((tm, D_MODEL), BF16),
            pltpu.VMEM((tm, D_FF), BF16),
            pltpu.VMEM((SUBLANES, 2 * D_FF), F32),
        ],
        compiler_params=pltpu.CompilerParams(
            dimension_semantics=("arbitrary",), vmem_limit_bytes=VMEM_LIMIT_BYTES),
        name="mix_ffn",
    )(x2d, g_attn, w_gate, o_fox, o_diff, o_stick, w_bf, w_bd, w_bs, w_out,
      g, w_up, conv_w, conv_b, w_down, g_final)


def _rope_tables(seq_len):
    half = ROPE_DIM // 2
    pos = jnp.arange(seq_len, dtype=F32)
    inv_freq = ROPE_THETA ** (-jnp.arange(0, ROPE_DIM, 2, dtype=F32) / ROPE_DIM)
    ang = pos[:, None] * inv_freq[None, :]
    cos, sin = jnp.cos(ang), jnp.sin(ang)
    ones = jnp.ones((seq_len, HEAD_DIM - ROPE_DIM), F32)
    zeros = jnp.zeros((seq_len, HEAD_DIM - ROPE_DIM), F32)
    zh = jnp.zeros((seq_len, half), F32)
    cos_h = jnp.concatenate([cos, cos, ones], axis=-1)
    sup_h = jnp.concatenate([zh, sin, zeros], axis=-1)
    sdn_h = jnp.concatenate([-sin, zh, zeros], axis=-1)
    rep = LANES // HEAD_DIM
    return tuple(jnp.tile(t, (1, rep)) for t in (cos_h, sup_h, sdn_h))


def _query_scale():
    scale = [1.0] * N_SLAB
    for off, width in ((OFF_FQ, W_FOX), (OFF_DQ, W_DIFF_QK), (OFF_SQ, W_STICK)):
        scale[off:off + width] = [QK_SCALE * LOG2_E] * width
    return jnp.asarray(scale, F32).reshape(1, N_SLAB)


def _forget_weight(w_in_b, forget_bias):
    w_f = jnp.tile(w_in_b[:, :, IN_OFF_FORGET:IN_OFF_FORGET + H_FOX], (1, 1, 3))
    w_f = jnp.pad(w_f, ((0, 0), (0, 0), (0, LANES - 3 * H_FOX)))
    bias = jnp.pad(jnp.tile(forget_bias.astype(F32), (1, 3)), ((0, 0), (0, LANES - 3 * H_FOX)))
    return w_f, bias


def kernel(x, attn_norm_g, w_in, forget_bias, lam_q1, lam_k1, lam_q2, lam_k2, diff_subln_g,
           w_br_fox, w_br_diff, w_br_stick, w_out, ffn_norm_g, w_up, conv_w, conv_b, w_down,
           final_norm_g):
    b, seq_len, d = x.shape
    depth = w_in.shape[0]
    n_tok = b * seq_len
    cos_t, sup_t, sdn_t = _rope_tables(seq_len)
    x2d = x.reshape(n_tok, d)
    g_final = final_norm_g.reshape(1, d)
    w_in_b = w_in.astype(BF16)
    q_scale = _query_scale()
    w_f, f_bias = _forget_weight(w_in_b, forget_bias)
    w_gate = w_in_b[:, :, IN_OFF_GATE:]
    bf16_weights = [w.astype(BF16) for w in (w_br_fox, w_br_diff, w_br_stick, w_out)]
    w_up_b, w_down_b = w_up.astype(BF16), w_down.astype(BF16)

    for l in range(depth):
        lam_init = 0.8 - 0.6 * math.exp(-0.3 * l)
        slab2d, decay = _inproj(x2d, attn_norm_g[l].reshape(1, d), w_in_b, q_scale, w_f,
                                f_bias[l].reshape(1, LANES), cos_t, sup_t, sdn_t, seq_len, l)
        slab = slab2d.reshape(b, seq_len, N_SLAB)

        o_fox = _fox_attention(slab, decay.reshape(b, seq_len, LANES))
        lam_params = jnp.stack([lam_q1[l], lam_k1[l], lam_q2[l], lam_k2[l]]).astype(F32)
        o_diff = _diff_attention(slab, lam_params, diff_subln_g[l].reshape(1, -1), lam_init)
        o_stick = _stick_attention(slab)

        x2d = _mix_ffn(x2d, attn_norm_g[l].reshape(1, d), w_gate,
                       o_fox.reshape(n_tok, -1), o_diff.reshape(n_tok, -1), o_stick.reshape(n_tok, -1),
                       *bf16_weights, ffn_norm_g[l].reshape(1, d), w_up_b,
                       conv_w[l], conv_b[l].reshape(1, -1), w_down_b, g_final,
                       seq_len, l, final=(l == depth - 1))
    return x2d.reshape(b, seq_len, d)
```

```python
import functools
import math

import jax
import jax.numpy as jnp
from jax import lax
from jax.experimental import pallas as pl
from jax.experimental.pallas import tpu as pltpu

F32 = jnp.float32
BF16 = jnp.bfloat16

D_MODEL = 1024
HEAD_DIM = 64
H_FOX = 4
H_DIFF = 4
H_STICK = 4
W_FOX = H_FOX * HEAD_DIM
W_DIFF_QK = 2 * H_DIFF * HEAD_DIM
W_DIFF = H_DIFF * 2 * HEAD_DIM
W_STICK = H_STICK * HEAD_DIM
N_BRANCH = 3
ROPE_THETA = 500000.0
ROPE_DIM = HEAD_DIM // 4
D_FF = 2816
CONV_WIDTH = 3
NORM_EPS = 1e-6
QK_SCALE = HEAD_DIM ** -0.5
LOG2_E = math.log2(math.e)

LANES = 128
SUBLANES = 8
VMEM_LIMIT_BYTES = 56 * 1024 * 1024

OFF_FQ = 0
OFF_FK = OFF_FQ + W_FOX
OFF_FV = OFF_FK + W_FOX
OFF_DQ = OFF_FV + W_FOX
OFF_DK = OFF_DQ + W_DIFF_QK
OFF_DV = OFF_DK + W_DIFF_QK
OFF_SQ = OFF_DV + W_DIFF
OFF_SK = OFF_SQ + W_STICK
OFF_SV = OFF_SK + W_STICK
N_SLAB = OFF_SV + W_STICK
IN_OFF_FORGET = N_SLAB
IN_OFF_GATE = IN_OFF_FORGET + H_FOX

TOKEN_TILE = 512
ATTN_TILE = 256
SUFFIX_TILE = 256
Q_CHUNKS = 4
ATTN_GROUP = 2
_STATE_ROWS = ATTN_GROUP * Q_CHUNKS * 2 * ATTN_TILE
SOFTMAX_OFF_KEYS = 1024
STICK_OFF_KEYS = 512
FF_CHUNK = 256


def _log_sigmoid(z):
    return jnp.minimum(z, 0.0) - jnp.log(1.0 + jnp.exp(-jnp.abs(z)))


def _rms_scale(x):
    return lax.rsqrt(jnp.mean(x * x, axis=-1, keepdims=True) + NORM_EPS)


def _split_bf16(v, n_parts):
    parts = []
    rem = v
    for _ in range(n_parts):
        p = rem.astype(BF16)
        parts.append(p)
        rem = rem - p.astype(F32)
    return parts


def _dot_nt(a, b):
    return lax.dot_general(a, b, (((1,), (1,)), ((), ())), preferred_element_type=F32)


def _dot(a, b):
    return jnp.dot(a, b, preferred_element_type=F32)


_PROJ_CHUNKS = (
    [(OFF_FQ, 3 * W_FOX // 2, "plain"), (OFF_FQ + 3 * W_FOX // 2, 3 * W_FOX // 2, "plain")]
    + [(OFF_DQ, W_DIFF_QK, "rope"), (OFF_DK, W_DIFF_QK, "rope"), (OFF_DV, W_DIFF, "plain")]
    + [(OFF_SQ, 3 * W_STICK // 2, "plain"), (OFF_SQ + 3 * W_STICK // 2, 3 * W_STICK // 2, "plain")]
)


def _rope_lanes(t, cos, sin_up, sin_dn):
    return (t * cos + pltpu.roll(t, ROPE_DIM // 2, 1) * sin_up
            + pltpu.roll(t, LANES - ROPE_DIM // 2, 1) * sin_dn)


def _inproj_kernel(tiles_per_seq, x_ref, g_ref, w_ref, scale_ref, wf_ref, fb_ref, tri_ref, cos_ref,
                   sup_ref, sdn_ref, slab_ref, decay_ref, xn_ref, carry_ref):
    @pl.when(pl.program_id(0) % tiles_per_seq == 0)
    def _():
        carry_ref[...] = jnp.zeros(carry_ref.shape, F32)

    def project(chunks):
        for off, width, kind in chunks:
            res = _dot(xn_ref[...], w_ref[:, off:off + width]) * scale_ref[:, off:off + width]
            if kind == "rope":
                cos, sup, sdn = cos_ref[...], sup_ref[...], sdn_ref[...]
                for c in range(width // LANES):
                    blk = _rope_lanes(res[:, c * LANES:(c + 1) * LANES], cos, sup, sdn)
                    slab_ref[:, off + c * LANES:off + (c + 1) * LANES] = blk.astype(BF16)
            else:
                slab_ref[:, off:off + width] = res.astype(BF16)

    x = x_ref[...]
    xn_ref[...] = ((x * _rms_scale(x)) * g_ref[...]).astype(BF16)
    log_f = _log_sigmoid(_dot(xn_ref[...], wf_ref[...]) + fb_ref[...])
    project(_PROJ_CHUNKS[:3])
    tri = tri_ref[...]
    c = carry_ref[0:1, :]
    for part in _split_bf16(log_f, 3):
        c = c + _dot(tri, part)
    project(_PROJ_CHUNKS[3:])
    tm = c.shape[0]
    carry_ref[...] = jnp.broadcast_to(c[tm - 1:tm, :], carry_ref.shape)
    hi, mid, lo = _split_bf16(c * (-LOG2_E), 3)
    lane = lax.broadcasted_iota(jnp.int32, c.shape, 1)
    part = jnp.where(lane < H_FOX, hi, jnp.where(lane < 2 * H_FOX, mid, lo))
    decay_ref[...] = jnp.where(lane < 3 * H_FOX, part, jnp.zeros_like(part))


def _layer_resident(stacked, layer):
    return pl.BlockSpec((None,) + stacked.shape[1:], lambda i: (layer, 0, 0),
                        pipeline_mode=pl.Buffered(1))


def _inproj(x2d, g, w_in_b, q_scale, w_f, f_bias, cos_t, sup_t, sdn_t, seq_len, layer):
    n_tok = x2d.shape[0]
    tm = TOKEN_TILE
    tiles_per_seq = seq_len // tm
    const = lambda i: (0, 0)
    rows = lambda i: (i, 0)
    rope_map = lambda i: (i % tiles_per_seq, 0)
    tri = jnp.tril(jnp.ones((tm, tm), BF16))
    return pl.pallas_call(
        functools.partial(_inproj_kernel, tiles_per_seq),
        grid=(n_tok // tm,),
        in_specs=[
            pl.BlockSpec((tm, D_MODEL), rows),
            pl.BlockSpec((1, D_MODEL), const),
            pl.BlockSpec((None, D_MODEL, N_SLAB), lambda i: (layer, 0, 0), pipeline_mode=pl.Buffered(1)),
            pl.BlockSpec((1, N_SLAB), const),
            _layer_resident(w_f, layer),
            pl.BlockSpec((1, LANES), const),
            pl.BlockSpec((tm, tm), const, pipeline_mode=pl.Buffered(1)),
            pl.BlockSpec((tm, LANES), rope_map),
            pl.BlockSpec((tm, LANES), rope_map),
            pl.BlockSpec((tm, LANES), rope_map),
        ],
        out_specs=[
            pl.BlockSpec((tm, N_SLAB), rows),
            pl.BlockSpec((tm, LANES), rows),
        ],
        out_shape=[
            jax.ShapeDtypeStruct((n_tok, N_SLAB), BF16),
            jax.ShapeDtypeStruct((n_tok, LANES), BF16),
        ],
        scratch_shapes=[pltpu.VMEM((tm, D_MODEL), BF16), pltpu.VMEM((SUBLANES, LANES), F32)],
        compiler_params=pltpu.CompilerParams(
            dimension_semantics=("arbitrary",), vmem_limit_bytes=VMEM_LIMIT_BYTES),
        name="inproj",
    )(x2d, g, w_in_b, q_scale, w_f, f_bias, tri, cos_t, sup_t, sdn_t)


def _stack_query_halves(q_ref, qm_ref):
    t = ATTN_TILE
    lane = lax.broadcasted_iota(jnp.int32, (t, LANES), 1)
    for g in range(ATTN_GROUP):
        for r in range(Q_CHUNKS):
            q = q_ref[g, r * t:(r + 1) * t, :]
            zero = jnp.zeros_like(q)
            qm_ref[_half_rows(g, r, 0), 0:LANES] = jnp.where(lane < HEAD_DIM, q, zero)
            qm_ref[_half_rows(g, r, 1), 0:LANES] = jnp.where(lane >= HEAD_DIM, q, zero)


def _key_before_query(strict):
    query = lax.broadcasted_iota(jnp.int32, (ATTN_TILE, ATTN_TILE), 0)
    key = lax.broadcasted_iota(jnp.int32, (ATTN_TILE, ATTN_TILE), 1)
    return key < query if strict else key <= query


def _causal_key_tiles(scores_fn, update_fn, off_keys, descending):
    t = ATTN_TILE
    block_start = Q_CHUNKS * t * pl.program_id(2)
    n_off = pl.program_id(2) * (Q_CHUNKS * t // off_keys)

    def step(ks, chunks):
        update_fn(ks, chunks, scores_fn(ks, chunks))

    def off_diagonal(jj, carry):
        j = n_off - 1 - jj if descending else jj
        step(pl.ds(pl.multiple_of(j * off_keys, off_keys), off_keys), _visible_chunks(None))
        return carry

    def diagonal(d):
        step(pl.ds(pl.multiple_of(block_start + d * t, t), t), _visible_chunks(d))

    if descending:
        for d in reversed(range(Q_CHUNKS)):
            diagonal(d)
        lax.fori_loop(0, n_off, off_diagonal, 0)
    else:
        lax.fori_loop(0, n_off, off_diagonal, 0)
        for d in range(Q_CHUNKS):
            diagonal(d)


def _visible_chunks(diag_chunk):
    first = 0 if diag_chunk is None else diag_chunk
    return [(g, r, r == diag_chunk) for r in range(first, Q_CHUNKS) for g in range(ATTN_GROUP)]


def _half_rows(g, r, half):
    block = (g * Q_CHUNKS + r) * 2 + half
    return slice(block * ATTN_TILE, (block + 1) * ATTN_TILE)


def _chunk_scores(qm_ref, k_tiles, chunks):
    return [_dot_nt(qm_ref[_half_rows(g, r, 0).start:_half_rows(g, r, 1).stop, :], k_tiles[g])
            for g, r, _ in chunks]


def _softmax_update(s, on_diagonal, v_aug, rows, m_ref, acc_ref):
    if on_diagonal:
        s = jnp.where(_key_before_query(strict=False), s, -jnp.inf)
    m_prev = m_ref[rows, :]
    m_new = jnp.maximum(m_prev, jnp.max(s, axis=-1, keepdims=True))
    p = jnp.exp2(s - jnp.concatenate([m_new] * (s.shape[1] // LANES), axis=1)).astype(BF16)
    alpha = jnp.exp2(m_prev - m_new)
    if acc_ref.shape[1] != LANES:
        alpha = jnp.concatenate([alpha] * (acc_ref.shape[1] // LANES), axis=1)
    acc_ref[rows, :] = alpha * acc_ref[rows, :] + _dot(p, v_aug)
    m_ref[rows, :] = m_new


def _init_softmax_state(m_ref, acc_ref):
    m_ref[...] = jnp.full(m_ref.shape, -jnp.inf, F32)
    acc_ref[...] = jnp.zeros(acc_ref.shape, F32)


def _fox_kernel(q_ref, k_ref, v_ref, decay_ref, o_ref, qm_ref, m_ref, acc_ref):
    t = ATTN_TILE
    _stack_query_halves(q_ref, qm_ref)
    _init_softmax_state(m_ref, acc_ref)
    lane = lax.broadcasted_iota(jnp.int32, (t, LANES), 1)
    for half in range(2):
        head = 2 * pl.program_id(1) + half
        own = (lane == head) | (lane == head + H_FOX) | (lane == head + 2 * H_FOX)
        ones_in_own_lanes = jnp.where(own, 1.0, 0.0).astype(BF16)
        for g in range(ATTN_GROUP):
            for r in range(Q_CHUNKS):
                qm_ref[_half_rows(g, r, half), LANES:2 * LANES] = ones_in_own_lanes

    def scores(ks, chunks):
        k_aug = [jnp.concatenate([k_ref[g, ks, :], decay_ref[g, ks, :]], axis=1)
                 for g in range(ATTN_GROUP)]
        return _chunk_scores(qm_ref, k_aug, chunks)

    def update(ks, chunks, score_list):
        v_aug = []
        for g in range(ATTN_GROUP):
            v = v_ref[g, ks, :]
            one = jnp.ones_like(v)
            lane_k = lax.broadcasted_iota(jnp.int32, v.shape, 1)
            v_aug.append([jnp.where(lane_k < HEAD_DIM, v, one), jnp.where(lane_k >= HEAD_DIM, v, one)])
        for (g, r, on_diagonal), s in zip(chunks, score_list):
            for half in range(2):
                _softmax_update(s[half * t:(half + 1) * t], on_diagonal, v_aug[g][half],
                                _half_rows(g, r, half), m_ref, acc_ref)

    _causal_key_tiles(scores, update, SOFTMAX_OFF_KEYS, descending=False)
    for g in range(ATTN_GROUP):
        for r in range(Q_CHUNKS):
            a0 = acc_ref[_half_rows(g, r, 0), :]
            a1 = acc_ref[_half_rows(g, r, 1), :]
            num = jnp.where(lane < HEAD_DIM, a0, a1)
            den = jnp.where(lane < HEAD_DIM, pltpu.roll(a0, HEAD_DIM, 1), pltpu.roll(a1, HEAD_DIM, 1))
            o_ref[g, r * t:(r + 1) * t, :] = (num / den).astype(o_ref.dtype)


def _diff_kernel(lam_init, q_ref, k_ref, v_ref, lam_ref, g_ref, o_ref, qm_ref, m_ref, acc_ref):
    t = ATTN_TILE
    dv = 2 * HEAD_DIM
    _stack_query_halves(q_ref, qm_ref)
    _init_softmax_state(m_ref, acc_ref)

    def scores(ks, chunks):
        return _chunk_scores(qm_ref, [k_ref[g, ks, :] for g in range(ATTN_GROUP)], chunks)

    def update(ks, chunks, score_list):
        v_aug = []
        for g in range(ATTN_GROUP):
            v = v_ref[g, ks, :]
            v_aug.append(jnp.concatenate([v, jnp.ones_like(v)], axis=1))
        for (g, r, on_diagonal), s in zip(chunks, score_list):
            for half in range(2):
                _softmax_update(s[half * t:(half + 1) * t], on_diagonal, v_aug[g],
                                _half_rows(g, r, half), m_ref, acc_ref)

    _causal_key_tiles(scores, update, SOFTMAX_OFF_KEYS, descending=False)
    lp = lam_ref[...]
    lam = (jnp.exp(jnp.sum(lp[0:1] * lp[1:2], axis=-1, keepdims=True))
           - jnp.exp(jnp.sum(lp[2:3] * lp[3:4], axis=-1, keepdims=True)) + lam_init)
    for g in range(ATTN_GROUP):
        for r in range(Q_CHUNKS):
            a1 = acc_ref[_half_rows(g, r, 0), :]
            a2 = acc_ref[_half_rows(g, r, 1), :]
            o = a1[:, :dv] / a1[:, dv:] - lam * (a2[:, :dv] / a2[:, dv:])
            o = (o * _rms_scale(o)) * g_ref[...]
            o_ref[g, r * t:(r + 1) * t, :] = (o * (1.0 - lam_init)).astype(o_ref.dtype)


def _stick_kernel(q_ref, k_ref, v_ref, o_ref, qm_ref, suffix_ref, r_ref, acc_ref):
    t = ATTN_TILE
    _stack_query_halves(q_ref, qm_ref)
    r_ref[...] = jnp.zeros(r_ref.shape, F32)
    acc_ref[...] = jnp.zeros(acc_ref.shape, F32)
    st = SUFFIX_TILE
    row = lax.broadcasted_iota(jnp.int32, (2 * st, st), 0)
    col = lax.broadcasted_iota(jnp.int32, (2 * st, st), 1)
    suffix_ref[...] = jnp.where((row >= col) & ((row < st) | (row >= col + st)), -1.0, 0.0).astype(BF16)

    def scores(ks, chunks):
        return _chunk_scores(qm_ref, [k_ref[g, ks, :] for g in range(ATTN_GROUP)], chunks)

    def update(ks, chunks, logits):
        values = [v_ref[g, ks, :] for g in range(ATTN_GROUP)]
        suffix = suffix_ref[...]
        n_sub = values[0].shape[0] // st
        blocks = [(i, g, r, half, on_diagonal)
                  for i, (g, r, on_diagonal) in enumerate(chunks) for half in range(2)]
        z_blocks, sums = [], []
        for i, g, r, half, on_diagonal in blocks:
            z = logits[i][half * t:(half + 1) * t]
            neg_log_1m = jnp.maximum(z, 0.0) + jnp.log2(1.0 + jnp.exp2(-jnp.abs(z)))
            if on_diagonal:
                neg_log_1m = jnp.where(_key_before_query(strict=True), neg_log_1m, 0.0)
            z_blocks.append(z)
            sums.append([
                _dot(jnp.concatenate(_split_bf16(neg_log_1m[:, u * st:(u + 1) * st], 2), axis=1), suffix)
                for u in range(n_sub)])
        for (_, g, r, half, on_diagonal), z, s in zip(blocks, z_blocks, sums):
            rows = _half_rows(g, r, half)
            r_run = r_ref[rows, :]
            w = [None] * n_sub
            for u in reversed(range(n_sub)):
                w_u = jnp.exp2(z[:, u * st:(u + 1) * st] + s[u]
                               + jnp.concatenate([r_run] * (st // LANES), axis=1))
                if on_diagonal:
                    w_u = jnp.where(_key_before_query(strict=True)[:, u * st:(u + 1) * st], w_u, 0.0)
                w[u] = w_u.astype(BF16)
                r_run = r_run + jnp.broadcast_to(s[u][:, 0:1], (t, LANES))
            acc_ref[rows, :] += _dot(jnp.concatenate(w, axis=1), values[g])
            r_ref[rows, :] = r_run

    _causal_key_tiles(scores, update, STICK_OFF_KEYS, descending=True)
    lane = lax.broadcasted_iota(jnp.int32, (t, LANES), 1)
    for g in range(ATTN_GROUP):
        for r in range(Q_CHUNKS):
            a0 = acc_ref[_half_rows(g, r, 0), :]
            a1 = acc_ref[_half_rows(g, r, 1), :]
            o_ref[g, r * t:(r + 1) * t, :] = jnp.where(lane < HEAD_DIM, a0, a1).astype(o_ref.dtype)


def _attention_call(kernel, slab, extra_inputs, extra_specs, n_blocks, q_blk, k_blk, v_blk,
                    scratch, name, query_lanes=LANES):
    b, seq_len, _ = slab.shape
    t = ATTN_TILE
    tq = Q_CHUNKS * t
    grp = ATTN_GROUP
    q_spec = pl.BlockSpec((grp, tq, LANES), lambda bi, h, i: (bi, i, q_blk + h))
    k_spec = pl.BlockSpec((grp, seq_len, LANES), lambda bi, h, i: (bi, 0, k_blk + h))
    v_spec = pl.BlockSpec((grp, seq_len, LANES), lambda bi, h, i: (bi, 0, v_blk + h))
    return pl.pallas_call(
        kernel,
        grid=(b // grp, n_blocks, seq_len // tq),
        in_specs=[q_spec, k_spec, v_spec] + extra_specs,
        out_specs=pl.BlockSpec((grp, tq, LANES), lambda bi, h, i: (bi, i, h)),
        out_shape=jax.ShapeDtypeStruct((b, seq_len, n_blocks * LANES), BF16),
        scratch_shapes=[pltpu.VMEM((_STATE_ROWS, query_lanes), BF16)] + scratch,
        compiler_params=pltpu.CompilerParams(
            dimension_semantics=("arbitrary", "arbitrary", "arbitrary"),
            vmem_limit_bytes=VMEM_LIMIT_BYTES),
        name=name,
    )(slab, slab, slab, *extra_inputs)


def _softmax_scratch(acc_lanes):
    return [pltpu.VMEM((_STATE_ROWS, LANES), F32), pltpu.VMEM((_STATE_ROWS, acc_lanes), F32)]


def _fox_attention(slab, decay):
    decay_spec = pl.BlockSpec((ATTN_GROUP,) + decay.shape[1:], lambda bi, h, i: (bi, 0, 0))
    return _attention_call(
        _fox_kernel, slab, [decay], [decay_spec], H_FOX // 2,
        OFF_FQ // LANES, OFF_FK // LANES, OFF_FV // LANES, _softmax_scratch(LANES), "fox_attention",
        query_lanes=2 * LANES)


def _diff_attention(slab, lam_params, subln_g, lam_init):
    const = lambda bi, h, i: (0, 0)
    specs = [pl.BlockSpec(lam_params.shape, const), pl.BlockSpec(subln_g.shape, const)]
    return _attention_call(
        functools.partial(_diff_kernel, lam_init), slab, [lam_params, subln_g], specs, H_DIFF,
        OFF_DQ // LANES, OFF_DK // LANES, OFF_DV // LANES, _softmax_scratch(2 * LANES),
        "diff_attention")


def _stick_attention(slab):
    st = SUFFIX_TILE
    scratch = [pltpu.VMEM((2 * st, st), BF16), pltpu.VMEM((_STATE_ROWS, LANES), F32),
               pltpu.VMEM((_STATE_ROWS, LANES), F32)]
    return _attention_call(
        _stick_kernel, slab, [], [], H_STICK // 2,
        OFF_SQ // LANES, OFF_SK // LANES, OFF_SV // LANES, scratch, "stick_attention")


def _shift_rows(u, prev, n):
    rolled = pltpu.roll(u, n, 0)
    head = rolled[:SUBLANES]
    row = lax.broadcasted_iota(jnp.int32, head.shape, 0)
    for r in range(n):
        src = SUBLANES - n + r
        head = jnp.where(row == r, prev[src:src + 1], head)
    return jnp.concatenate([head, rolled[SUBLANES:]], axis=0)


def _mix_ffn_kernel(tiles_per_seq, final, x_ref, ga_ref, wg_ref, of_ref, od_ref, os_ref, wf_ref,
                    wd_ref, ws_ref, wo_ref, g_ref, wup_ref, cw_ref, cb_ref, wdn_ref, gfin_ref,
                    y_ref, hn_ref, act_ref, tail_ref):
    tm = x_ref.shape[0]
    d = D_MODEL

    @pl.when(pl.program_id(0) % tiles_per_seq == 0)
    def _():
        tail_ref[...] = jnp.zeros(tail_ref.shape, F32)

    x = x_ref[...]
    hn_ref[...] = ((x * _rms_scale(x)) * ga_ref[...]).astype(BF16)
    mixed = None
    for branch, (o_ref, w_ref) in enumerate(((of_ref, wf_ref), (od_ref, wd_ref), (os_ref, ws_ref))):
        gate = jax.nn.sigmoid(_dot(hn_ref[...], wg_ref[:, branch * d:(branch + 1) * d]))
        term = gate * _dot(o_ref[...], w_ref[...])
        mixed = term if mixed is None else mixed + term
    x = x + _dot(mixed.astype(BF16), wo_ref[...])

    hn_ref[...] = ((x * _rms_scale(x)) * g_ref[...]).astype(BF16)
    for c in range(D_FF // FF_CHUNK):
        conv = []
        for part in range(2):
            cols = slice(part * D_FF + c * FF_CHUNK, part * D_FF + (c + 1) * FF_CHUNK)
            u = _dot(hn_ref[...], wup_ref[:, cols])
            prev = tail_ref[:, cols]
            tail_ref[:, cols] = u[tm - SUBLANES:]
            conv.append(cw_ref[0:1, cols] * _shift_rows(u, prev, 2)
                        + cw_ref[1:2, cols] * _shift_rows(u, prev, 1)
                        + cw_ref[2:3, cols] * u + cb_ref[:, cols])
        gate, val = conv
        act_ref[:, c * FF_CHUNK:(c + 1) * FF_CHUNK] = (gate * jax.nn.sigmoid(gate) * val).astype(BF16)
    y = x + _dot(act_ref[...], wdn_ref[...])
    if final:
        y = (y * _rms_scale(y)) * gfin_ref[...]
    y_ref[...] = y


def _mix_ffn(x2d, g_attn, w_gate, o_fox, o_diff, o_stick, w_bf, w_bd, w_bs, w_out,
             g, w_up, conv_w, conv_b, w_down, g_final, seq_len, layer, final):
    n_tok = x2d.shape[0]
    tm = TOKEN_TILE
    const = lambda i: (0, 0)
    rows = lambda i: (i, 0)
    resident = lambda w: _layer_resident(w, layer)
    return pl.pallas_call(
        functools.partial(_mix_ffn_kernel, seq_len // tm, final),
        grid=(n_tok // tm,),
        in_specs=[
            pl.BlockSpec((tm, D_MODEL), rows),
            pl.BlockSpec((1, D_MODEL), const),
            resident(w_gate),
            pl.BlockSpec((tm, W_FOX), rows),
            pl.BlockSpec((tm, W_DIFF), rows),
            pl.BlockSpec((tm, W_STICK), rows),
            resident(w_bf), resident(w_bd), resident(w_bs), resident(w_out),
            pl.BlockSpec((1, D_MODEL), const),
            resident(w_up),
            pl.BlockSpec(conv_w.shape, const),
            pl.BlockSpec(conv_b.shape, const),
            resident(w_down),
            pl.BlockSpec((1, D_MODEL), const),
        ],
        out_specs=pl.BlockSpec((tm, D_MODEL), rows),
        out_shape=jax.ShapeDtypeStruct((n_tok, D_MODEL), F32),
        scratch_shapes=[
            pltpu.VMEM((tm, D_MODEL), BF16),
            pltpu.VMEM((tm, D_FF), BF16),
            pltpu.VMEM((SUBLANES, 2 * D_FF), F32),
        ],
        compiler_params=pltpu.CompilerParams(
            dimension_semantics=("arbitrary",), vmem_limit_bytes=VMEM_LIMIT_BYTES),
        name="mix_ffn",
    )(x2d, g_attn, w_gate, o_fox, o_diff, o_stick, w_bf, w_bd, w_bs, w_out,
      g, w_up, conv_w, conv_b, w_down, g_final)


def _rope_tables(seq_len):
    half = ROPE_DIM // 2
    pos = jnp.arange(seq_len, dtype=F32)
    inv_freq = ROPE_THETA ** (-jnp.arange(0, ROPE_DIM, 2, dtype=F32) / ROPE_DIM)
    ang = pos[:, None] * inv_freq[None, :]
    cos, sin = jnp.cos(ang), jnp.sin(ang)
    ones = jnp.ones((seq_len, HEAD_DIM - ROPE_DIM), F32)
    zeros = jnp.zeros((seq_len, HEAD_DIM - ROPE_DIM), F32)
    zh = jnp.zeros((seq_len, half), F32)
    cos_h = jnp.concatenate([cos, cos, ones], axis=-1)
    sup_h = jnp.concatenate([zh, sin, zeros], axis=-1)
    sdn_h = jnp.concatenate([-sin, zh, zeros], axis=-1)
    rep = LANES // HEAD_DIM
    return tuple(jnp.tile(t, (1, rep)) for t in (cos_h, sup_h, sdn_h))


def _query_scale():
    scale = [1.0] * N_SLAB
    for off, width in ((OFF_FQ, W_FOX), (OFF_DQ, W_DIFF_QK), (OFF_SQ, W_STICK)):
        scale[off:off + width] = [QK_SCALE * LOG2_E] * width
    return jnp.asarray(scale, F32).reshape(1, N_SLAB)


def _forget_weight(w_in_b, forget_bias):
    w_f = jnp.tile(w_in_b[:, :, IN_OFF_FORGET:IN_OFF_FORGET + H_FOX], (1, 1, 3))
    w_f = jnp.pad(w_f, ((0, 0), (0, 0), (0, LANES - 3 * H_FOX)))
    bias = jnp.pad(jnp.tile(forget_bias.astype(F32), (1, 3)), ((0, 0), (0, LANES - 3 * H_FOX)))
    return w_f, bias


def kernel(x, attn_norm_g, w_in, forget_bias, lam_q1, lam_k1, lam_q2, lam_k2, diff_subln_g,
           w_br_fox, w_br_diff, w_br_stick, w_out, ffn_norm_g, w_up, conv_w, conv_b, w_down,
           final_norm_g):
    b, seq_len, d = x.shape
    depth = w_in.shape[0]
    n_tok = b * seq_len
    assert d == D_MODEL and w_in.shape[2] == IN_OFF_GATE + N_BRANCH * D_MODEL
    assert conv_w.shape[1] == CONV_WIDTH == 3 and w_down.shape[1] == D_FF
    assert seq_len % (Q_CHUNKS * ATTN_TILE) == 0 and seq_len % TOKEN_TILE == 0 and b % ATTN_GROUP == 0
    cos_t, sup_t, sdn_t = _rope_tables(seq_len)
    x2d = x.reshape(n_tok, d)
    g_final = final_norm_g.reshape(1, d)
    w_in_b = w_in.astype(BF16)
    q_scale = _query_scale()
    w_f, f_bias = _forget_weight(w_in_b, forget_bias)
    w_gate = w_in_b[:, :, IN_OFF_GATE:]
    bf16_weights = [w.astype(BF16) for w in (w_br_fox, w_br_diff, w_br_stick, w_out)]
    w_up_b, w_down_b = w_up.astype(BF16), w_down.astype(BF16)

    for l in range(depth):
        lam_init = 0.8 - 0.6 * math.exp(-0.3 * l)
        slab2d, decay = _inproj(x2d, attn_norm_g[l].reshape(1, d), w_in_b, q_scale, w_f,
                                f_bias[l].reshape(1, LANES), cos_t, sup_t, sdn_t, seq_len, l)
        slab = slab2d.reshape(b, seq_len, N_SLAB)

        o_fox = _fox_attention(slab, decay.reshape(b, seq_len, LANES))
        lam_params = jnp.stack([lam_q1[l], lam_k1[l], lam_q2[l], lam_k2[l]]).astype(F32)
        o_diff = _diff_attention(slab, lam_params, diff_subln_g[l].reshape(1, -1), lam_init)
        o_stick = _stick_attention(slab)

        x2d = _mix_ffn(x2d, attn_norm_g[l].reshape(1, d), w_gate,
                       o_fox.reshape(n_tok, -1), o_diff.reshape(n_tok, -1), o_stick.reshape(n_tok, -1),
                       *bf16_weights, ffn_norm_g[l].reshape(1, d), w_up_b,
                       conv_w[l], conv_b[l].reshape(1, -1), w_down_b, g_final,
                       seq_len, l, final=(l == depth - 1))
    return x2d.reshape(b, seq_len, d)
```

```python
import functools
import math

import jax
import jax.numpy as jnp
from jax import lax
from jax.experimental import pallas as pl
from jax.experimental.pallas import tpu as pltpu

F32 = jnp.float32
BF16 = jnp.bfloat16

D_MODEL = 1024
HEAD_DIM = 64
H_FOX = 4
H_DIFF = 4
H_STICK = 4
W_FOX = H_FOX * HEAD_DIM
W_DIFF_QK = 2 * H_DIFF * HEAD_DIM
W_DIFF = H_DIFF * 2 * HEAD_DIM
W_STICK = H_STICK * HEAD_DIM
N_BRANCH = 3
ROPE_THETA = 500000.0
ROPE_DIM = HEAD_DIM // 4
D_FF = 2816
CONV_WIDTH = 3
NORM_EPS = 1e-6
QK_SCALE = HEAD_DIM ** -0.5
LOG2_E = math.log2(math.e)

LANES = 128
SUBLANES = 8
VMEM_LIMIT_BYTES = 56 * 1024 * 1024

OFF_FQ = 0
OFF_FK = OFF_FQ + W_FOX
OFF_FV = OFF_FK + W_FOX
OFF_DQ = OFF_FV + W_FOX
OFF_DK = OFF_DQ + W_DIFF_QK
OFF_DV = OFF_DK + W_DIFF_QK
OFF_SQ = OFF_DV + W_DIFF
OFF_SK = OFF_SQ + W_STICK
OFF_SV = OFF_SK + W_STICK
N_SLAB = OFF_SV + W_STICK
IN_OFF_FORGET = N_SLAB
IN_OFF_GATE = IN_OFF_FORGET + H_FOX

TOKEN_TILE = 512
ATTN_TILE = 256
SUFFIX_TILE = 256
Q_CHUNKS = 4
ATTN_GROUP = 2
_STATE_ROWS = ATTN_GROUP * Q_CHUNKS * 2 * ATTN_TILE
SOFTMAX_OFF_KEYS = 512
STICK_OFF_KEYS = 512
FF_CHUNK = 256


def _log_sigmoid(z):
    return jnp.minimum(z, 0.0) - jnp.log(1.0 + jnp.exp(-jnp.abs(z)))


def _rms_scale(x):
    return lax.rsqrt(jnp.mean(x * x, axis=-1, keepdims=True) + NORM_EPS)


def _split_bf16(v, n_parts):
    parts = []
    rem = v
    for _ in range(n_parts):
        p = rem.astype(BF16)
        parts.append(p)
        rem = rem - p.astype(F32)
    return parts


def _dot_nt(a, b):
    return lax.dot_general(a, b, (((1,), (1,)), ((), ())), preferred_element_type=F32)


def _dot(a, b):
    return jnp.dot(a, b, preferred_element_type=F32)


_PROJ_CHUNKS = (
    [(OFF_FQ, 3 * W_FOX // 2, "plain"), (OFF_FQ + 3 * W_FOX // 2, 3 * W_FOX // 2, "plain")]
    + [(OFF_DQ, W_DIFF_QK, "rope"), (OFF_DK, W_DIFF_QK, "rope"), (OFF_DV, W_DIFF, "plain")]
    + [(OFF_SQ, 3 * W_STICK // 2, "plain"), (OFF_SQ + 3 * W_STICK // 2, 3 * W_STICK // 2, "plain")]
)


def _rope_lanes(t, cos, sin_up, sin_dn):
    return (t * cos + pltpu.roll(t, ROPE_DIM // 2, 1) * sin_up
            + pltpu.roll(t, LANES - ROPE_DIM // 2, 1) * sin_dn)


def _inproj_kernel(tiles_per_seq, x_ref, g_ref, w_ref, scale_ref, wf_ref, fb_ref, tri_ref, cos_ref,
                   sup_ref, sdn_ref, slab_ref, decay_ref, xn_ref, carry_ref):
    @pl.when(pl.program_id(0) % tiles_per_seq == 0)
    def _():
        carry_ref[...] = jnp.zeros(carry_ref.shape, F32)

    def project(chunks):
        for off, width, kind in chunks:
            res = _dot(xn_ref[...], w_ref[:, off:off + width]) * scale_ref[:, off:off + width]
            if kind == "rope":
                cos, sup, sdn = cos_ref[...], sup_ref[...], sdn_ref[...]
                for c in range(width // LANES):
                    blk = _rope_lanes(res[:, c * LANES:(c + 1) * LANES], cos, sup, sdn)
                    slab_ref[:, off + c * LANES:off + (c + 1) * LANES] = blk.astype(BF16)
            else:
                slab_ref[:, off:off + width] = res.astype(BF16)

    x = x_ref[...]
    xn_ref[...] = ((x * _rms_scale(x)) * g_ref[...]).astype(BF16)
    log_f = _log_sigmoid(_dot(xn_ref[...], wf_ref[...]) + fb_ref[...])
    project(_PROJ_CHUNKS[:3])
    tri = tri_ref[...]
    c = carry_ref[0:1, :]
    for part in _split_bf16(log_f, 3):
        c = c + _dot(tri, part)
    project(_PROJ_CHUNKS[3:])
    tm = c.shape[0]
    carry_ref[...] = jnp.broadcast_to(c[tm - 1:tm, :], carry_ref.shape)
    hi, mid, lo = _split_bf16(c * (-LOG2_E), 3)
    lane = lax.broadcasted_iota(jnp.int32, c.shape, 1)
    part = jnp.where(lane < H_FOX, hi, jnp.where(lane < 2 * H_FOX, mid, lo))
    decay_ref[...] = jnp.where(lane < 3 * H_FOX, part, jnp.zeros_like(part))


def _layer_resident(stacked, layer):
    return pl.BlockSpec((None,) + stacked.shape[1:], lambda i: (layer, 0, 0),
                        pipeline_mode=pl.Buffered(1))


def _inproj(x2d, g, w_in_b, q_scale, w_f, f_bias, cos_t, sup_t, sdn_t, seq_len, layer):
    n_tok = x2d.shape[0]
    tm = TOKEN_TILE
    tiles_per_seq = seq_len // tm
    const = lambda i: (0, 0)
    rows = lambda i: (i, 0)
    rope_map = lambda i: (i % tiles_per_seq, 0)
    tri = jnp.tril(jnp.ones((tm, tm), BF16))
    return pl.pallas_call(
        functools.partial(_inproj_kernel, tiles_per_seq),
        grid=(n_tok // tm,),
        in_specs=[
            pl.BlockSpec((tm, D_MODEL), rows),
            pl.BlockSpec((1, D_MODEL), const),
            pl.BlockSpec((None, D_MODEL, N_SLAB), lambda i: (layer, 0, 0), pipeline_mode=pl.Buffered(1)),
            pl.BlockSpec((1, N_SLAB), const),
            _layer_resident(w_f, layer),
            pl.BlockSpec((1, LANES), const),
            pl.BlockSpec((tm, tm), const, pipeline_mode=pl.Buffered(1)),
            pl.BlockSpec((tm, LANES), rope_map),
            pl.BlockSpec((tm, LANES), rope_map),
            pl.BlockSpec((tm, LANES), rope_map),
        ],
        out_specs=[
            pl.BlockSpec((tm, N_SLAB), rows),
            pl.BlockSpec((tm, LANES), rows),
        ],
        out_shape=[
            jax.ShapeDtypeStruct((n_tok, N_SLAB), BF16),
            jax.ShapeDtypeStruct((n_tok, LANES), BF16),
        ],
        scratch_shapes=[pltpu.VMEM((tm, D_MODEL), BF16), pltpu.VMEM((SUBLANES, LANES), F32)],
        compiler_params=pltpu.CompilerParams(
            dimension_semantics=("arbitrary",), vmem_limit_bytes=VMEM_LIMIT_BYTES),
        name="inproj",
    )(x2d, g, w_in_b, q_scale, w_f, f_bias, tri, cos_t, sup_t, sdn_t)


def _stack_query_halves(q_ref, qm_ref):
    t = ATTN_TILE
    lane = lax.broadcasted_iota(jnp.int32, (t, LANES), 1)
    for g in range(ATTN_GROUP):
        for r in range(Q_CHUNKS):
            q = q_ref[g, r * t:(r + 1) * t, :]
            zero = jnp.zeros_like(q)
            qm_ref[_half_rows(g, r, 0), 0:LANES] = jnp.where(lane < HEAD_DIM, q, zero)
            qm_ref[_half_rows(g, r, 1), 0:LANES] = jnp.where(lane >= HEAD_DIM, q, zero)


def _key_before_query(strict):
    query = lax.broadcasted_iota(jnp.int32, (ATTN_TILE, ATTN_TILE), 0)
    key = lax.broadcasted_iota(jnp.int32, (ATTN_TILE, ATTN_TILE), 1)
    return key < query if strict else key <= query


def _causal_key_tiles(scores_fn, update_fn, off_keys, descending):
    t = ATTN_TILE
    block_start = Q_CHUNKS * t * pl.program_id(2)
    n_off = pl.program_id(2) * (Q_CHUNKS * t // off_keys)

    def step(ks, chunks):
        update_fn(ks, chunks, scores_fn(ks, chunks))

    def off_diagonal(jj, carry):
        j = n_off - 1 - jj if descending else jj
        step(pl.ds(pl.multiple_of(j * off_keys, off_keys), off_keys), _visible_chunks(None))
        return carry

    def diagonal(d):
        step(pl.ds(pl.multiple_of(block_start + d * t, t), t), _visible_chunks(d))

    if descending:
        for d in reversed(range(Q_CHUNKS)):
            diagonal(d)
        lax.fori_loop(0, n_off, off_diagonal, 0)
    else:
        lax.fori_loop(0, n_off, off_diagonal, 0)
        for d in range(Q_CHUNKS):
            diagonal(d)


def _visible_chunks(diag_chunk):
    first = 0 if diag_chunk is None else diag_chunk
    return [(g, r, r == diag_chunk) for r in range(first, Q_CHUNKS) for g in range(ATTN_GROUP)]


def _half_rows(g, r, half):
    block = (g * Q_CHUNKS + r) * 2 + half
    return slice(block * ATTN_TILE, (block + 1) * ATTN_TILE)


def _chunk_scores(qm_ref, k_tiles, chunks):
    return [_dot_nt(qm_ref[_half_rows(g, r, 0).start:_half_rows(g, r, 1).stop, :], k_tiles[g])
            for g, r, _ in chunks]


def _softmax_update(s, on_diagonal, v_aug, rows, m_ref, acc_ref):
    if on_diagonal:
        s = jnp.where(_key_before_query(strict=False), s, -jnp.inf)
    m_prev = m_ref[rows, :]
    m_new = jnp.maximum(m_prev, jnp.max(s, axis=-1, keepdims=True))
    p = jnp.exp2(s - jnp.concatenate([m_new] * (s.shape[1] // LANES), axis=1)).astype(BF16)
    alpha = jnp.exp2(m_prev - m_new)
    if acc_ref.shape[1] != LANES:
        alpha = jnp.concatenate([alpha] * (acc_ref.shape[1] // LANES), axis=1)
    acc_ref[rows, :] = alpha * acc_ref[rows, :] + _dot(p, v_aug)
    m_ref[rows, :] = m_new


def _init_softmax_state(m_ref, acc_ref):
    m_ref[...] = jnp.full(m_ref.shape, -jnp.inf, F32)
    acc_ref[...] = jnp.zeros(acc_ref.shape, F32)


def _fox_kernel(q_ref, k_ref, v_ref, decay_ref, o_ref, qm_ref, m_ref, acc_ref):
    t = ATTN_TILE
    _stack_query_halves(q_ref, qm_ref)
    _init_softmax_state(m_ref, acc_ref)
    lane = lax.broadcasted_iota(jnp.int32, (t, LANES), 1)
    for half in range(2):
        head = 2 * pl.program_id(1) + half
        own = (lane == head) | (lane == head + H_FOX) | (lane == head + 2 * H_FOX)
        ones_in_own_lanes = jnp.where(own, 1.0, 0.0).astype(BF16)
        for g in range(ATTN_GROUP):
            for r in range(Q_CHUNKS):
                qm_ref[_half_rows(g, r, half), LANES:2 * LANES] = ones_in_own_lanes

    def scores(ks, chunks):
        k_aug = [jnp.concatenate([k_ref[g, ks, :], decay_ref[g, ks, :]], axis=1)
                 for g in range(ATTN_GROUP)]
        return _chunk_scores(qm_ref, k_aug, chunks)

    def update(ks, chunks, score_list):
        v_aug = []
        for g in range(ATTN_GROUP):
            v = v_ref[g, ks, :]
            one = jnp.ones_like(v)
            lane_k = lax.broadcasted_iota(jnp.int32, v.shape, 1)
            v_aug.append([jnp.where(lane_k < HEAD_DIM, v, one), jnp.where(lane_k >= HEAD_DIM, v, one)])
        for (g, r, on_diagonal), s in zip(chunks, score_list):
            for half in range(2):
                _softmax_update(s[half * t:(half + 1) * t], on_diagonal, v_aug[g][half],
                                _half_rows(g, r, half), m_ref, acc_ref)

    _causal_key_tiles(scores, update, SOFTMAX_OFF_KEYS, descending=False)
    for g in range(ATTN_GROUP):
        for r in range(Q_CHUNKS):
            a0 = acc_ref[_half_rows(g, r, 0), :]
            a1 = acc_ref[_half_rows(g, r, 1), :]
            num = jnp.where(lane < HEAD_DIM, a0, a1)
            den = jnp.where(lane < HEAD_DIM, pltpu.roll(a0, HEAD_DIM, 1), pltpu.roll(a1, HEAD_DIM, 1))
            o_ref[g, r * t:(r + 1) * t, :] = (num / den).astype(o_ref.dtype)


def _diff_kernel(lam_init, q_ref, k_ref, v_ref, lam_ref, g_ref, o_ref, qm_ref, m_ref, acc_ref):
    t = ATTN_TILE
    dv = 2 * HEAD_DIM
    _stack_query_halves(q_ref, qm_ref)
    _init_softmax_state(m_ref, acc_ref)

    def scores(ks, chunks):
        return _chunk_scores(qm_ref, [k_ref[g, ks, :] for g in range(ATTN_GROUP)], chunks)

    def update(ks, chunks, score_list):
        v_aug = []
        for g in range(ATTN_GROUP):
            v = v_ref[g, ks, :]
            v_aug.append(jnp.concatenate([v, jnp.ones_like(v)], axis=1))
        for (g, r, on_diagonal), s in zip(chunks, score_list):
            for half in range(2):
                _softmax_update(s[half * t:(half + 1) * t], on_diagonal, v_aug[g],
                                _half_rows(g, r, half), m_ref, acc_ref)

    _causal_key_tiles(scores, update, SOFTMAX_OFF_KEYS, descending=False)
    lp = lam_ref[...]
    lam = (jnp.exp(jnp.sum(lp[0:1] * lp[1:2], axis=-1, keepdims=True))
           - jnp.exp(jnp.sum(lp[2:3] * lp[3:4], axis=-1, keepdims=True)) + lam_init)
    for g in range(ATTN_GROUP):
        for r in range(Q_CHUNKS):
            a1 = acc_ref[_half_rows(g, r, 0), :]
            a2 = acc_ref[_half_rows(g, r, 1), :]
            o = a1[:, :dv] / a1[:, dv:] - lam * (a2[:, :dv] / a2[:, dv:])
            o = (o * _rms_scale(o)) * g_ref[...]
            o_ref[g, r * t:(r + 1) * t, :] = (o * (1.0 - lam_init)).astype(o_ref.dtype)


def _stick_kernel(q_ref, k_ref, v_ref, o_ref, qm_ref, suffix_ref, r_ref, acc_ref):
    t = ATTN_TILE
    _stack_query_halves(q_ref, qm_ref)
    r_ref[...] = jnp.zeros(r_ref.shape, F32)
    acc_ref[...] = jnp.zeros(acc_ref.shape, F32)
    st = SUFFIX_TILE
    row = lax.broadcasted_iota(jnp.int32, (2 * st, st), 0)
    col = lax.broadcasted_iota(jnp.int32, (2 * st, st), 1)
    suffix_ref[...] = jnp.where((row >= col) & ((row < st) | (row >= col + st)), -1.0, 0.0).astype(BF16)

    def scores(ks, chunks):
        return _chunk_scores(qm_ref, [k_ref[g, ks, :] for g in range(ATTN_GROUP)], chunks)

    def update(ks, chunks, logits):
        values = [v_ref[g, ks, :] for g in range(ATTN_GROUP)]
        suffix = suffix_ref[...]
        n_sub = values[0].shape[0] // st
        blocks = [(i, g, r, half, on_diagonal)
                  for i, (g, r, on_diagonal) in enumerate(chunks) for half in range(2)]
        z_blocks, sums = [], []
        for i, g, r, half, on_diagonal in blocks:
            z = logits[i][half * t:(half + 1) * t]
            neg_log_1m = jnp.maximum(z, 0.0) + jnp.log2(1.0 + jnp.exp2(-jnp.abs(z)))
            if on_diagonal:
                neg_log_1m = jnp.where(_key_before_query(strict=True), neg_log_1m, 0.0)
            z_blocks.append(z)
            sums.append([
                _dot(jnp.concatenate(_split_bf16(neg_log_1m[:, u * st:(u + 1) * st], 2), axis=1), suffix)
                for u in range(n_sub)])
        for (_, g, r, half, on_diagonal), z, s in zip(blocks, z_blocks, sums):
            rows = _half_rows(g, r, half)
            r_run = r_ref[rows, :]
            w = [None] * n_sub
            for u in reversed(range(n_sub)):
                w_u = jnp.exp2(z[:, u * st:(u + 1) * st] + s[u]
                               + jnp.concatenate([r_run] * (st // LANES), axis=1))
                if on_diagonal:
                    w_u = jnp.where(_key_before_query(strict=True)[:, u * st:(u + 1) * st], w_u, 0.0)
                w[u] = w_u.astype(BF16)
                r_run = r_run + jnp.broadcast_to(s[u][:, 0:1], (t, LANES))
            acc_ref[rows, :] += _dot(jnp.concatenate(w, axis=1), values[g])
            r_ref[rows, :] = r_run

    _causal_key_tiles(scores, update, STICK_OFF_KEYS, descending=True)
    lane = lax.broadcasted_iota(jnp.int32, (t, LANES), 1)
    for g in range(ATTN_GROUP):
        for r in range(Q_CHUNKS):
            a0 = acc_ref[_half_rows(g, r, 0), :]
            a1 = acc_ref[_half_rows(g, r, 1), :]
            o_ref[g, r * t:(r + 1) * t, :] = jnp.where(lane < HEAD_DIM, a0, a1).astype(o_ref.dtype)


def _attention_call(kernel, slab, extra_inputs, extra_specs, n_blocks, q_blk, k_blk, v_blk,
                    scratch, name, query_lanes=LANES):
    b, seq_len, _ = slab.shape
    t = ATTN_TILE
    tq = Q_CHUNKS * t
    grp = ATTN_GROUP
    q_spec = pl.BlockSpec((grp, tq, LANES), lambda bi, h, i: (bi, i, q_blk + h))
    k_spec = pl.BlockSpec((grp, seq_len, LANES), lambda bi, h, i: (bi, 0, k_blk + h))
    v_spec = pl.BlockSpec((grp, seq_len, LANES), lambda bi, h, i: (bi, 0, v_blk + h))
    return pl.pallas_call(
        kernel,
        grid=(b // grp, n_blocks, seq_len // tq),
        in_specs=[q_spec, k_spec, v_spec] + extra_specs,
        out_specs=pl.BlockSpec((grp, tq, LANES), lambda bi, h, i: (bi, i, h)),
        out_shape=jax.ShapeDtypeStruct((b, seq_len, n_blocks * LANES), BF16),
        scratch_shapes=[pltpu.VMEM((_STATE_ROWS, query_lanes), BF16)] + scratch,
        compiler_params=pltpu.CompilerParams(
            dimension_semantics=("arbitrary", "arbitrary", "arbitrary"),
            vmem_limit_bytes=VMEM_LIMIT_BYTES),
        name=name,
    )(slab, slab, slab, *extra_inputs)


def _softmax_scratch(acc_lanes):
    return [pltpu.VMEM((_STATE_ROWS, LANES), F32), pltpu.VMEM((_STATE_ROWS, acc_lanes), F32)]


def _fox_attention(slab, decay):
    decay_spec = pl.BlockSpec((ATTN_GROUP,) + decay.shape[1:], lambda bi, h, i: (bi, 0, 0))
    return _attention_call(
        _fox_kernel, slab, [decay], [decay_spec], H_FOX // 2,
        OFF_FQ // LANES, OFF_FK // LANES, OFF_FV // LANES, _softmax_scratch(LANES), "fox_attention",
        query_lanes=2 * LANES)


def _diff_attention(slab, lam_params, subln_g, lam_init):
    const = lambda bi, h, i: (0, 0)
    specs = [pl.BlockSpec(lam_params.shape, const), pl.BlockSpec(subln_g.shape, const)]
    return _attention_call(
        functools.partial(_diff_kernel, lam_init), slab, [lam_params, subln_g], specs, H_DIFF,
        OFF_DQ // LANES, OFF_DK // LANES, OFF_DV // LANES, _softmax_scratch(2 * LANES),
        "diff_attention")


def _stick_attention(slab):
    st = SUFFIX_TILE
    scratch = [pltpu.VMEM((2 * st, st), BF16), pltpu.VMEM((_STATE_ROWS, LANES), F32),
               pltpu.VMEM((_STATE_ROWS, LANES), F32)]
    return _attention_call(
        _stick_kernel, slab, [], [], H_STICK // 2,
        OFF_SQ // LANES, OFF_SK // LANES, OFF_SV // LANES, scratch, "stick_attention")


def _shift_rows(u, prev, n):
    rolled = pltpu.roll(u, n, 0)
    head = rolled[:SUBLANES]
    row = lax.broadcasted_iota(jnp.int32, head.shape, 0)
    for r in range(n):
        src = SUBLANES - n + r
        head = jnp.where(row == r, prev[src:src + 1], head)
    return jnp.concatenate([head, rolled[SUBLANES:]], axis=0)


def _mix_ffn_kernel(tiles_per_seq, final, x_ref, ga_ref, wg_ref, of_ref, od_ref, os_ref, wf_ref,
                    wd_ref, ws_ref, wo_ref, g_ref, wup_ref, cw_ref, cb_ref, wdn_ref, gfin_ref,
                    y_ref, hn_ref, act_ref, tail_ref):
    tm = x_ref.shape[0]
    d = D_MODEL

    @pl.when(pl.program_id(0) % tiles_per_seq == 0)
    def _():
        tail_ref[...] = jnp.zeros(tail_ref.shape, F32)

    x = x_ref[...]
    hn_ref[...] = ((x * _rms_scale(x)) * ga_ref[...]).astype(BF16)
    mixed = None
    for branch, (o_ref, w_ref) in enumerate(((of_ref, wf_ref), (od_ref, wd_ref), (os_ref, ws_ref))):
        gate = jax.nn.sigmoid(_dot(hn_ref[...], wg_ref[:, branch * d:(branch + 1) * d]))
        term = gate * _dot(o_ref[...], w_ref[...])
        mixed = term if mixed is None else mixed + term
    x = x + _dot(mixed.astype(BF16), wo_ref[...])

    hn_ref[...] = ((x * _rms_scale(x)) * g_ref[...]).astype(BF16)
    for c in range(D_FF // FF_CHUNK):
        conv = []
        for part in range(2):
            cols = slice(part * D_FF + c * FF_CHUNK, part * D_FF + (c + 1) * FF_CHUNK)
            u = _dot(hn_ref[...], wup_ref[:, cols])
            prev = tail_ref[:, cols]
            tail_ref[:, cols] = u[tm - SUBLANES:]
            conv.append(cw_ref[0:1, cols] * _shift_rows(u, prev, 2)
                        + cw_ref[1:2, cols] * _shift_rows(u, prev, 1)
                        + cw_ref[2:3, cols] * u + cb_ref[:, cols])
        gate, val = conv
        act_ref[:, c * FF_CHUNK:(c + 1) * FF_CHUNK] = (gate * jax.nn.sigmoid(gate) * val).astype(BF16)
    y = x + _dot(act_ref[...], wdn_ref[...])
    if final:
        y = (y * _rms_scale(y)) * gfin_ref[...]
    y_ref[...] = y


def _mix_ffn(x2d, g_attn, w_gate, o_fox, o_diff, o_stick, w_bf, w_bd, w_bs, w_out,
             g, w_up, conv_w, conv_b, w_down, g_final, seq_len, layer, final):
    n_tok = x2d.shape[0]
    tm = TOKEN_TILE
    const = lambda i: (0, 0)
    rows = lambda i: (i, 0)
    resident = lambda w: _layer_resident(w, layer)
    return pl.pallas_call(
        functools.partial(_mix_ffn_kernel, seq_len // tm, final),
        grid=(n_tok // tm,),
        in_specs=[
            pl.BlockSpec((tm, D_MODEL), rows),
            pl.BlockSpec((1, D_MODEL), const),
            resident(w_gate),
            pl.BlockSpec((tm, W_FOX), rows),
            pl.BlockSpec((tm, W_DIFF), rows),
            pl.BlockSpec((tm, W_STICK), rows),
            resident(w_bf), resident(w_bd), resident(w_bs), resident(w_out),
            pl.BlockSpec((1, D_MODEL), const),
            resident(w_up),
            pl.BlockSpec(conv_w.shape, const),
            pl.BlockSpec(conv_b.shape, const),
            resident(w_down),
            pl.BlockSpec((1, D_MODEL), const),
        ],
        out_specs=pl.BlockSpec((tm, D_MODEL), rows),
        out_shape=jax.ShapeDtypeStruct((n_tok, D_MODEL), F32),
        scratch_shapes=[
            pltpu.VMEM((tm, D_MODEL), BF16),
            pltpu.VMEM((tm, D_FF), BF16),
            pltpu.VMEM((SUBLANES, 2 * D_FF), F32),
        ],
        compiler_params=pltpu.CompilerParams(
            dimension_semantics=("arbitrary",), vmem_limit_bytes=VMEM_LIMIT_BYTES),
        name="mix_ffn",
    )(x2d, g_attn, w_gate, o_fox, o_diff, o_stick, w_bf, w_bd, w_bs, w_out,
      g, w_up, conv_w, conv_b, w_down, g_final)


def _rope_tables(seq_len):
    half = ROPE_DIM // 2
    pos = jnp.arange(seq_len, dtype=F32)
    inv_freq = ROPE_THETA ** (-jnp.arange(0, ROPE_DIM, 2, dtype=F32) / ROPE_DIM)
    ang = pos[:, None] * inv_freq[None, :]
    cos, sin = jnp.cos(ang), jnp.sin(ang)
    ones = jnp.ones((seq_len, HEAD_DIM - ROPE_DIM), F32)
    zeros = jnp.zeros((seq_len, HEAD_DIM - ROPE_DIM), F32)
    zh = jnp.zeros((seq_len, half), F32)
    cos_h = jnp.concatenate([cos, cos, ones], axis=-1)
    sup_h = jnp.concatenate([zh, sin, zeros], axis=-1)
    sdn_h = jnp.concatenate([-sin, zh, zeros], axis=-1)
    rep = LANES // HEAD_DIM
    return tuple(jnp.tile(t, (1, rep)) for t in (cos_h, sup_h, sdn_h))


def _query_scale():
    scale = [1.0] * N_SLAB
    for off, width in ((OFF_FQ, W_FOX), (OFF_DQ, W_DIFF_QK), (OFF_SQ, W_STICK)):
        scale[off:off + width] = [QK_SCALE * LOG2_E] * width
    return jnp.asarray(scale, F32).reshape(1, N_SLAB)


def _forget_weight(w_in_b, forget_bias):
    w_f = jnp.tile(w_in_b[:, :, IN_OFF_FORGET:IN_OFF_FORGET + H_FOX], (1, 1, 3))
    w_f = jnp.pad(w_f, ((0, 0), (0, 0), (0, LANES - 3 * H_FOX)))
    bias = jnp.pad(jnp.tile(forget_bias.astype(F32), (1, 3)), ((0, 0), (0, LANES - 3 * H_FOX)))
    return w_f, bias


def kernel(x, attn_norm_g, w_in, forget_bias, lam_q1, lam_k1, lam_q2, lam_k2, diff_subln_g,
           w_br_fox, w_br_diff, w_br_stick, w_out, ffn_norm_g, w_up, conv_w, conv_b, w_down,
           final_norm_g):
    b, seq_len, d = x.shape
    depth = w_in.shape[0]
    n_tok = b * seq_len
    assert d == D_MODEL and w_in.shape[2] == IN_OFF_GATE + N_BRANCH * D_MODEL
    assert conv_w.shape[1] == CONV_WIDTH == 3 and w_down.shape[1] == D_FF
    assert seq_len % (Q_CHUNKS * ATTN_TILE) == 0 and seq_len % TOKEN_TILE == 0 and b % ATTN_GROUP == 0
    cos_t, sup_t, sdn_t = _rope_tables(seq_len)
    x2d = x.reshape(n_tok, d)
    g_final = final_norm_g.reshape(1, d)
    w_in_b = w_in.astype(BF16)
    q_scale = _query_scale()
    w_f, f_bias = _forget_weight(w_in_b, forget_bias)
    w_gate = w_in_b[:, :, IN_OFF_GATE:]
    bf16_weights = [w.astype(BF16) for w in (w_br_fox, w_br_diff, w_br_stick, w_out)]
    w_up_b, w_down_b = w_up.astype(BF16), w_down.astype(BF16)

    for l in range(depth):
        lam_init = 0.8 - 0.6 * math.exp(-0.3 * l)
        slab2d, decay = _inproj(x2d, attn_norm_g[l].reshape(1, d), w_in_b, q_scale, w_f,
                                f_bias[l].reshape(1, LANES), cos_t, sup_t, sdn_t, seq_len, l)
        slab = slab2d.reshape(b, seq_len, N_SLAB)

        o_fox = _fox_attention(slab, decay.reshape(b, seq_len, LANES))
        lam_params = jnp.stack([lam_q1[l], lam_k1[l], lam_q2[l], lam_k2[l]]).astype(F32)
        o_diff = _diff_attention(slab, lam_params, diff_subln_g[l].reshape(1, -1), lam_init)
        o_stick = _stick_attention(slab)

        x2d = _mix_ffn(x2d, attn_norm_g[l].reshape(1, d), w_gate,
                       o_fox.reshape(n_tok, -1), o_diff.reshape(n_tok, -1), o_stick.reshape(n_tok, -1),
                       *bf16_weights, ffn_norm_g[l].reshape(1, d), w_up_b,
                       conv_w[l], conv_b[l].reshape(1, -1), w_down_b, g_final,
                       seq_len, l, final=(l == depth - 1))
    return x2d.reshape(b, seq_len, d)
```
